```python
import math
import jax, jax.numpy as jnp
from jax import lax
import numpy as np

D_MODEL = 2048
BATCH = 2
SEQ = 8192
DEPTH = 4

N_BRANCH = 4
BRANCH_WIDTH = 512

GLA_HEADS = 4
GLA_DK = 64
GLA_DV = 128
GLA_LOWRANK = 16
GLA_NORMALIZER = 16.0
GLA_CHUNK = 64

LRU_WIDTH = 512
LRU_BLOCKS = 4
LRU_BLOCK_DIM = LRU_WIDTH // LRU_BLOCKS
LRU_CONV = 4
LRU_C = 8.0

NSA_HEADS = 4
NSA_DH = 128
NSA_CMP_LEN = 32
NSA_CMP_STRIDE = 16
NSA_SEL_BLOCK = 64
NSA_SEL_TOPK = 16
NSA_WINDOW = 512
NSA_QBLOCK = 128

RWKV_HEADS = 8
RWKV_DH = 64
RWKV_WIDTH = RWKV_HEADS * RWKV_DH
RWKV_W_LORA = 96
RWKV_A_LORA = 96
RWKV_G_LORA = 256
RWKV_SPLITS = (RWKV_WIDTH, RWKV_WIDTH, RWKV_WIDTH, RWKV_W_LORA, RWKV_A_LORA, RWKV_G_LORA)
RWKV_IN = sum(RWKV_SPLITS)

NUM_BUCKETS = 32
MAX_DISTANCE = 128

D_FF = 5632
FFN_CONV = 3

IN_SPLITS = (
    GLA_HEADS * GLA_DK, GLA_HEADS * GLA_DK, GLA_HEADS * GLA_DV, GLA_HEADS * GLA_DV, GLA_LOWRANK,
    LRU_WIDTH, LRU_WIDTH,
    NSA_HEADS * NSA_DH, 6 * NSA_DH, 3 * NSA_HEADS,
    RWKV_IN,
    N_BRANCH * D_MODEL,
)
IN_TOTAL = sum(IN_SPLITS)
NEG_BIG = 1e9

kernel_name = 'hybrid_gla_rglru_nsa_rwkv7_convffn'


def rms_norm(x, g, eps=1e-6):
    xf = x.astype(jnp.float32)
    y = xf * lax.rsqrt(jnp.mean(xf * xf, axis=-1, keepdims=True) + eps)
    return (y * g.astype(jnp.float32)).astype(x.dtype)


def causal_dwconv(x, w, b):
    k, c = w.shape
    y = lax.conv_general_dilated(x, w[:, None, :].astype(x.dtype), window_strides=(1,), padding=[(k - 1, 0)],
                                 dimension_numbers=('NWC', 'WIO', 'NWC'), feature_group_count=c)
    return y + b.astype(x.dtype)


def split_cols(t, widths):
    parts, start = [], 0
    for w in widths:
        parts.append(t[..., start:start + w])
        start += w
    return parts


def masked_softmax(logits, mask):
    logits = jnp.where(mask, logits.astype(jnp.float32), -NEG_BIG)
    m = jnp.max(logits, axis=-1, keepdims=True)
    p = jnp.where(mask, jnp.exp(logits - m), 0.0)
    return p / jnp.maximum(jnp.sum(p, axis=-1, keepdims=True), 1e-30)


def t5_bucket(dist):
    n = jnp.maximum(dist, 0)
    max_exact = NUM_BUCKETS // 2
    nf = jnp.maximum(n, 1).astype(jnp.float32)
    large = max_exact + (jnp.log(nf / max_exact) / math.log(MAX_DISTANCE / max_exact)
                         * (NUM_BUCKETS - max_exact)).astype(jnp.int32)
    large = jnp.minimum(large, NUM_BUCKETS - 1)
    return jnp.where(n < max_exact, n, large)


def gla_mixer(q, k, v, g, lr, w_gk, b_gk, out_gain):
    B, S, _ = q.shape
    H, dk, dv, C = GLA_HEADS, GLA_DK, GLA_DV, GLA_CHUNK
    nc = S // C
    f32 = jnp.float32
    log_a = jax.nn.log_sigmoid((lr @ w_gk + b_gk).astype(f32)) / GLA_NORMALIZER

    def chunks(t, d):
        return t.astype(f32).reshape(B, nc, C, H, d).transpose(1, 0, 3, 2, 4)

    qc = chunks(q, dk) * dk ** -0.5
    kc = chunks(k, dk)
    vc = chunks(v, dv)
    bc = jnp.cumsum(chunks(log_a, dk), axis=3)
    causal = jnp.tril(jnp.ones((C, C), dtype=bool))[:, :, None]

    def step(state, inp):
        qi, ki, vi, bi = inp
        o_inter = jnp.einsum('bhtd,bhdv->bhtv', qi * jnp.exp(bi), state)
        decay = jnp.exp(jnp.where(causal, bi[:, :, :, None, :] - bi[:, :, None, :, :], -jnp.inf))
        scores = jnp.einsum('bhtd,bhsd,bhtsd->bhts', qi, ki, decay)
        o = o_inter + jnp.einsum('bhts,bhsv->bhtv', scores, vi)
        b_last = bi[:, :, -1:, :]
        state = jnp.exp(b_last[:, :, 0, :, None]) * state + jnp.einsum('bhsd,bhsv->bhdv', ki * jnp.exp(b_last - bi), vi)
        return state, o

    s0 = jnp.zeros((B, H, dk, dv), f32)
    _, o = lax.scan(step, s0, (qc, kc, vc, bc))
    o = o.transpose(1, 0, 3, 2, 4).reshape(B, S, H, dv)
    o = rms_norm(o, out_gain) * jax.nn.silu(g.astype(f32)).reshape(B, S, H, dv)
    return o.reshape(B, S, H * dv).astype(q.dtype)


def rglru_mixer(xb, gb, conv_w, conv_b, w_a, b_a, w_i, b_i, lam):
    B, S, _ = xb.shape
    f32 = jnp.float32
    xc = causal_dwconv(xb, conv_w, conv_b)
    xh = xc.reshape(B, S, LRU_BLOCKS, LRU_BLOCK_DIM)
    r = jax.nn.sigmoid(jnp.einsum('bsni,nij->bsnj', xh, w_a).reshape(B, S, LRU_WIDTH) + b_a).astype(f32)
    i = jax.nn.sigmoid(jnp.einsum('bsni,nij->bsnj', xh, w_i).reshape(B, S, LRU_WIDTH) + b_i).astype(f32)
    log_a = LRU_C * r * jax.nn.log_sigmoid(lam.astype(f32))
    a = jnp.exp(log_a)
    u = jnp.sqrt(-jnp.expm1(2.0 * log_a)) * (i * xc.astype(f32))
    _, h = lax.associative_scan(lambda e, l: (e[0] * l[0], l[0] * e[1] + l[1]), (a, u), axis=1)
    return (h * jax.nn.gelu(gb.astype(f32))).astype(xb.dtype)


def nsa_mixer(q, kv, gate, rel_bias, cmp_pos, cmp_k1, cmp_k2, cmp_v1, cmp_v2, q_gain, k_gain):
    B, S, _ = q.shape
    H, dh = NSA_HEADS, NSA_DH
    L, D, SB, QB, W = NSA_CMP_LEN, NSA_CMP_STRIDE, NSA_SEL_BLOCK, NSA_QBLOCK, NSA_WINDOW
    f32 = jnp.float32
    scale = dh ** -0.5
    pos = jnp.arange(S)
    qh = rms_norm(q.reshape(B, S, H, dh), q_gain).transpose(0, 2, 1, 3)
    k_cmp, v_cmp, k_slc, v_slc, k_win, v_win = split_cols(kv, (dh,) * 6)

    n_cmp = (S - L) // D + 1
    cmp_start = jnp.arange(n_cmp) * D
    idx = cmp_start[:, None] + jnp.arange(L)[None, :]

    def compress(t, w1, w2):
        blocks = (t[:, idx, :] + cmp_pos).reshape(B, n_cmp, L * dh)
        return jax.nn.gelu(blocks @ w1) @ w2

    kc = rms_norm(compress(k_cmp, cmp_k1, cmp_k2), k_gain)
    vc = compress(v_cmp, cmp_v1, cmp_v2)
    dist_c = pos[:, None] - (cmp_start + L - 1)[None, :]
    bias_c = rel_bias[t5_bucket(dist_c)].transpose(2, 0, 1).astype(f32)
    logit_c = jnp.einsum('bhsd,bnd->bhsn', qh, kc).astype(f32) * scale + bias_c
    p_c = masked_softmax(logit_c, dist_c >= 0)
    o_cmp = jnp.einsum('bhsn,bnd->bhsd', p_c, vc.astype(f32))

    n_sel = S // SB
    top_k = min(NSA_SEL_TOPK, n_sel)
    sel_start = jnp.arange(n_sel) * SB
    overlap = ((cmp_start[:, None] < sel_start[None, :] + SB) &
               (cmp_start[:, None] + L > sel_start[None, :])).astype(f32)
    score = jnp.einsum('bhsn,nj->bsj', p_c, overlap)
    blk = jnp.arange(n_sel)[None, :]
    cur = (pos // SB)[:, None]
    forced = (blk == 0) | (blk == cur) | (blk == cur - 1)
    score = jnp.where(forced, NEG_BIG, jnp.where(sel_start[None, :] <= pos[:, None], score, -NEG_BIG))
    _, sel_idx = lax.top_k(score, top_k)

    k_blocks = rms_norm(k_slc, k_gain).reshape(B, n_sel, SB, dh)
    v_blocks = v_slc.reshape(B, n_sel, SB, dh)
    nq = S // QB
    q_blocks = qh.reshape(B, H, nq, QB, dh).transpose(2, 0, 1, 3, 4)
    idx_blocks = sel_idx.reshape(B, nq, QB, top_k).transpose(1, 0, 2, 3)
    pos_blocks = pos.reshape(nq, QB)
    gather = jax.vmap(lambda blocks, ids: blocks[ids])

    def select_block(args):
        qb, ib, pb = args
        kg = gather(k_blocks, ib).reshape(B, QB, top_k * SB, dh)
        vg = gather(v_blocks, ib).reshape(B, QB, top_k * SB, dh)
        kpos = (ib[..., None] * SB + jnp.arange(SB)).reshape(B, QB, top_k * SB)
        dist = pb[None, :, None] - kpos
        bias = rel_bias[t5_bucket(dist)].transpose(0, 3, 1, 2).astype(f32)
        logit = jnp.einsum('bhtd,btkd->bhtk', qb, kg).astype(f32) * scale + bias
        p = masked_softmax(logit, (dist >= 0)[:, None])
        return jnp.einsum('bhtk,btkd->bhtd', p, vg.astype(f32))

    o_sel = lax.map(select_block, (q_blocks, idx_blocks, pos_blocks))
    o_sel = o_sel.transpose(1, 2, 0, 3, 4).reshape(B, H, S, dh)

    nw = W // QB

    def band(t):
        tp = jnp.pad(t, ((0, 0), (W, 0), (0, 0))).reshape(B, nq + nw, QB, dh)
        return jnp.concatenate([tp[:, j:j + nq] for j in range(nw + 1)], axis=2)

    kw = band(rms_norm(k_win, k_gain))
    vw = band(v_win)
    kpos_w = jnp.arange(nq)[:, None] * QB - W + jnp.arange((nw + 1) * QB)[None, :]
    dist_w = pos_blocks[:, :, None] - kpos_w[:, None, :]
    mask_w = (dist_w >= 0) & (dist_w < W) & (kpos_w[:, None, :] >= 0)
    bias_w = rel_bias[t5_bucket(dist_w)].transpose(3, 0, 1, 2).astype(f32)
    logit_w = jnp.einsum('bhnqd,bnkd->bhnqk', qh.reshape(B, H, nq, QB, dh), kw).astype(f32) * scale + bias_w
    p_w = masked_softmax(logit_w, mask_w)
    o_win = jnp.einsum('bhnqk,bnkd->bhnqd', p_w, vw.astype(f32)).reshape(B, H, S, dh)

    g = jax.nn.sigmoid(gate.astype(f32)).reshape(B, S, 3, H).transpose(2, 0, 3, 1)[..., None]
    o = g[0] * o_cmp + g[1] * o_sel + g[2] * o_win
    return o.transpose(0, 2, 1, 3).reshape(B, S, H * dh).astype(q.dtype)


def rwkv7_mixer(feat, mu, w0, w_lora, a0, a_lora, g_lora, k_k, k_a, r_k, ln_w, ln_b):
    B, S, _ = feat.shape
    H, N = RWKV_HEADS, RWKV_DH
    f32 = jnp.float32
    out_dtype = feat.dtype
    feat = feat.astype(f32)
    prev = jnp.pad(feat, ((0, 0), (1, 0), (0, 0)))[:, :-1]
    xm = feat + (prev - feat) * mu
    r, k, v, xw, xa, xg = split_cols(xm, RWKV_SPLITS)
    log_w = -math.exp(-0.5) * jax.nn.sigmoid(w0 + jnp.tanh(xw) @ w_lora)
    a = jax.nn.sigmoid(a0 + xa @ a_lora)
    g = jax.nn.sigmoid(xg) @ g_lora
    heads = lambda t: t.reshape(B, S, H, N)
    kk = heads(k * k_k)
    kk = kk / jnp.maximum(jnp.linalg.norm(kk, axis=-1, keepdims=True), 1e-12)
    k = k * (1.0 + (a - 1.0) * k_a)
    tm = lambda t: t.transpose(1, 0, 2, 3)

    def step(state, inp):
        r_t, w_t, k_t, v_t, kk_t, a_t = inp
        sa = jnp.einsum('bhij,bhj->bhi', state, -kk_t)
        state = (state * w_t[:, :, None, :] + sa[..., None] * (kk_t * a_t)[:, :, None, :]
                 + v_t[..., None] * k_t[:, :, None, :])
        return state, jnp.einsum('bhij,bhj->bhi', state, r_t)

    s0 = jnp.zeros((B, H, N, N), f32)
    xs = (tm(heads(r)), tm(heads(jnp.exp(log_w))), tm(heads(k)), tm(heads(v)), tm(kk), tm(heads(a)))
    _, y = lax.scan(step, s0, xs)
    y = y.transpose(1, 0, 2, 3)
    mean = jnp.mean(y, axis=-1, keepdims=True)
    var = jnp.mean(jnp.square(y - mean), axis=-1, keepdims=True)
    y = (y - mean) * lax.rsqrt(var + 64e-5) * ln_w.reshape(H, N) + ln_b.reshape(H, N)
    y = y + jnp.sum(heads(r) * heads(k) * r_k, axis=-1, keepdims=True) * heads(v)
    return (y.reshape(B, S, H * N) * g).astype(out_dtype)


def setup_inputs(seed: int = 0) -> dict:
    key = jax.random.key(seed)
    ks = iter(jax.random.split(key, 40))
    f32 = jnp.float32

    def nrm(shape, scale):
        return jax.random.normal(next(ks), shape, f32) * scale

    def gain(shape):
        return 1.0 + nrm(shape, 0.05)

    Lh = DEPTH
    res_scale = (2 * DEPTH) ** -0.5
    x = nrm((BATCH, SEQ, D_MODEL), 1.0)
    rel_bias = nrm((NUM_BUCKETS, NSA_HEADS), 0.5)
    attn_norm = gain((Lh, D_MODEL))
    ffn_norm = gain((Lh, D_MODEL))
    w_in = nrm((Lh, D_MODEL, IN_TOTAL), D_MODEL ** -0.5)
    gla_w_gk = nrm((Lh, GLA_LOWRANK, GLA_HEADS * GLA_DK), GLA_LOWRANK ** -0.5)
    gla_b_gk = nrm((Lh, GLA_HEADS * GLA_DK), 0.5)
    gla_out_norm = gain((Lh, GLA_DV))
    lru_conv_w = nrm((Lh, LRU_CONV, LRU_WIDTH), LRU_CONV ** -0.5)
    lru_conv_b = nrm((Lh, LRU_WIDTH), 0.02)
    lru_w_a = nrm((Lh, LRU_BLOCKS, LRU_BLOCK_DIM, LRU_BLOCK_DIM), LRU_BLOCK_DIM ** -0.5)
    lru_b_a = nrm((Lh, LRU_WIDTH), 0.1)
    lru_w_i = nrm((Lh, LRU_BLOCKS, LRU_BLOCK_DIM, LRU_BLOCK_DIM), LRU_BLOCK_DIM ** -0.5)
    lru_b_i = nrm((Lh, LRU_WIDTH), 0.1)
    a_base = jax.random.uniform(next(ks), (Lh, LRU_WIDTH), f32, minval=0.9, maxval=0.999)
    s = a_base ** (1.0 / LRU_C)
    lru_lambda = jnp.log(s) - jnp.log1p(-s)
    nsa_cmp_pos = nrm((Lh, NSA_CMP_LEN, NSA_DH), 0.1)
    nsa_cmp_k1 = nrm((Lh, NSA_CMP_LEN * NSA_DH, NSA_DH), (NSA_CMP_LEN * NSA_DH) ** -0.5)
    nsa_cmp_k2 = nrm((Lh, NSA_DH, NSA_DH), NSA_DH ** -0.5)
    nsa_cmp_v1 = nrm((Lh, NSA_CMP_LEN * NSA_DH, NSA_DH), (NSA_CMP_LEN * NSA_DH) ** -0.5)
    nsa_cmp_v2 = nrm((Lh, NSA_DH, NSA_DH), NSA_DH ** -0.5)
    nsa_q_norm = gain((Lh, NSA_DH))
    nsa_k_norm = gain((Lh, NSA_DH))
    rwkv_mu = jax.random.uniform(next(ks), (Lh, RWKV_IN), f32)
    rwkv_w0 = nrm((Lh, RWKV_WIDTH), 1.0)
    rwkv_w_lora = nrm((Lh, RWKV_W_LORA, RWKV_WIDTH), RWKV_W_LORA ** -0.5)
    rwkv_a0 = nrm((Lh, RWKV_WIDTH), 0.5)
    rwkv_a_lora = nrm((Lh, RWKV_A_LORA, RWKV_WIDTH), RWKV_A_LORA ** -0.5)
    rwkv_g_lora = nrm((Lh, RWKV_G_LORA, RWKV_WIDTH), RWKV_G_LORA ** -0.5)
    rwkv_k_k = 0.85 + nrm((Lh, RWKV_WIDTH), 0.05)
    rwkv_k_a = gain((Lh, RWKV_WIDTH))
    rwkv_r_k = nrm((Lh, RWKV_HEADS, RWKV_DH), 0.1)
    rwkv_ln_w = gain((Lh, RWKV_WIDTH))
    rwkv_ln_b = nrm((Lh, RWKV_WIDTH), 0.02)
    w_branch = nrm((Lh, N_BRANCH, BRANCH_WIDTH, D_MODEL), BRANCH_WIDTH ** -0.5)
    w_out = nrm((Lh, D_MODEL, D_MODEL), D_MODEL ** -0.5 * res_scale)
    ffn_up = nrm((Lh, D_MODEL, 2 * D_FF), D_MODEL ** -0.5)
    ffn_conv_w = nrm((Lh, FFN_CONV, 2 * D_FF), FFN_CONV ** -0.5)
    ffn_conv_b = nrm((Lh, 2 * D_FF), 0.02)
    ffn_down = nrm((Lh, D_FF, D_MODEL), D_FF ** -0.5 * res_scale)
    return {'x': x, 'rel_bias': rel_bias, 'attn_norm': attn_norm, 'ffn_norm': ffn_norm, 'w_in': w_in,
            'gla_w_gk': gla_w_gk, 'gla_b_gk': gla_b_gk, 'gla_out_norm': gla_out_norm,
            'lru_conv_w': lru_conv_w, 'lru_conv_b': lru_conv_b, 'lru_w_a': lru_w_a, 'lru_b_a': lru_b_a,
            'lru_w_i': lru_w_i, 'lru_b_i': lru_b_i, 'lru_lambda': lru_lambda,
            'nsa_cmp_pos': nsa_cmp_pos, 'nsa_cmp_k1': nsa_cmp_k1, 'nsa_cmp_k2': nsa_cmp_k2,
            'nsa_cmp_v1': nsa_cmp_v1, 'nsa_cmp_v2': nsa_cmp_v2, 'nsa_q_norm': nsa_q_norm, 'nsa_k_norm': nsa_k_norm,
            'rwkv_mu': rwkv_mu, 'rwkv_w0': rwkv_w0, 'rwkv_w_lora': rwkv_w_lora, 'rwkv_a0': rwkv_a0,
            'rwkv_a_lora': rwkv_a_lora, 'rwkv_g_lora': rwkv_g_lora, 'rwkv_k_k': rwkv_k_k, 'rwkv_k_a': rwkv_k_a,
            'rwkv_r_k': rwkv_r_k, 'rwkv_ln_w': rwkv_ln_w, 'rwkv_ln_b': rwkv_ln_b,
            'w_branch': w_branch, 'w_out': w_out, 'ffn_up': ffn_up, 'ffn_conv_w': ffn_conv_w,
            'ffn_conv_b': ffn_conv_b, 'ffn_down': ffn_down}


def reference(x, rel_bias, attn_norm, ffn_norm, w_in, gla_w_gk, gla_b_gk, gla_out_norm,
              lru_conv_w, lru_conv_b, lru_w_a, lru_b_a, lru_w_i, lru_b_i, lru_lambda,
              nsa_cmp_pos, nsa_cmp_k1, nsa_cmp_k2, nsa_cmp_v1, nsa_cmp_v2, nsa_q_norm, nsa_k_norm,
              rwkv_mu, rwkv_w0, rwkv_w_lora, rwkv_a0, rwkv_a_lora, rwkv_g_lora, rwkv_k_k, rwkv_k_a,
              rwkv_r_k, rwkv_ln_w, rwkv_ln_b, w_branch, w_out, ffn_up, ffn_conv_w, ffn_conv_b, ffn_down):
    B, S, _ = x.shape
    for l in range(DEPTH):
        h = rms_norm(x, attn_norm[l])
        proj = h @ w_in[l]
        (gla_q, gla_k, gla_v, gla_g, gla_lr, lru_x, lru_g,
         nsa_q, nsa_kv, nsa_gate, rwkv_feat, merge_gate) = split_cols(proj, IN_SPLITS)
        y_a = gla_mixer(gla_q, gla_k, gla_v, gla_g, gla_lr, gla_w_gk[l], gla_b_gk[l], gla_out_norm[l])
        y_b = rglru_mixer(lru_x, lru_g, lru_conv_w[l], lru_conv_b[l], lru_w_a[l], lru_b_a[l],
                          lru_w_i[l], lru_b_i[l], lru_lambda[l])
        y_c = nsa_mixer(nsa_q, nsa_kv, nsa_gate, rel_bias, nsa_cmp_pos[l], nsa_cmp_k1[l], nsa_cmp_k2[l],
                        nsa_cmp_v1[l], nsa_cmp_v2[l], nsa_q_norm[l], nsa_k_norm[l])
        y_d = rwkv7_mixer(rwkv_feat, rwkv_mu[l], rwkv_w0[l], rwkv_w_lora[l], rwkv_a0[l], rwkv_a_lora[l],
                          rwkv_g_lora[l], rwkv_k_k[l], rwkv_k_a[l], rwkv_r_k[l], rwkv_ln_w[l], rwkv_ln_b[l])
        gates = jax.nn.sigmoid(merge_gate).reshape(B, S, N_BRANCH, D_MODEL)
        merged = jnp.zeros_like(x)
        for n, y in enumerate((y_a, y_b, y_c, y_d)):
            merged = merged + gates[:, :, n] * (y @ w_branch[l, n])
        x = x + merged @ w_out[l]
        h = rms_norm(x, ffn_norm[l])
        u = causal_dwconv(h @ ffn_up[l], ffn_conv_w[l], ffn_conv_b[l])
        u_gate, u_val = jnp.split(u, 2, axis=-1)
        x = x + (jax.nn.silu(u_gate) * u_val) @ ffn_down[l]
    return x
```

```python
import functools
import math

import numpy as np
import jax
import jax.numpy as jnp
from jax import lax
from jax.experimental import pallas as pl
from jax.experimental.pallas import tpu as pltpu

F32 = jnp.float32
BF16 = jnp.bfloat16
HI = lax.Precision.HIGHEST

LANE = 128
VMEM_LIMIT = 56 * 1024 * 1024

D_MODEL = 2048
N_BRANCH = 4
BRANCH_WIDTH = 512

GLA_HEADS, GLA_DK, GLA_DV, GLA_LOWRANK, GLA_NORMALIZER = 4, 64, 128, 16, 16.0
LRU_WIDTH, LRU_BLOCKS, LRU_CONV, LRU_C = 512, 4, 4, 8.0
NSA_HEADS, NSA_DH = 4, 128
NSA_CMP_LEN, NSA_CMP_STRIDE, NSA_SEL_BLOCK, NSA_SEL_TOPK, NSA_WINDOW, NSA_QBLOCK = 32, 16, 64, 16, 512, 128
RWKV_HEADS, RWKV_DH, RWKV_WIDTH = 8, 64, 512
RWKV_W_LORA, RWKV_A_LORA, RWKV_G_LORA = 96, 96, 256
NUM_BUCKETS, MAX_DISTANCE = 32, 128
D_FF, FFN_CONV = 5632, 3
NEG_BIG = 1e9
MASKED = -1e30

P_GATES, P_RWKV, P_LRU_X, P_LRU_G, P_NSA_Q, P_GLA_V, P_GLA_G = 0, 8192, 10240, 10752, 11264, 11776, 12288
P_GLA_Q, P_NSA_KV, P_GLA_K, P_GLA_LR, P_NSA_GATE, P_TOTAL = 12800, 13056, 13824, 14080, 14208, 14336
RW_R, RW_K, RW_V, RW_XW, RW_XA, RW_XG, RW_TOTAL = 0, 512, 1024, 1536, 1664, 1792, 2048

CHUNK = 64


def _cparams(*sem):
    return pltpu.CompilerParams(dimension_semantics=sem, vmem_limit_bytes=VMEM_LIMIT)


def _dot(a, b, prec=None):
    return jnp.dot(a, b, preferred_element_type=F32, precision=prec)


def _dot_nt(a, b, prec=None):
    return lax.dot_general(a, b, (((1,), (1,)), ((), ())), preferred_element_type=F32, precision=prec)


def _dot_tn(a, b, prec=None):
    return lax.dot_general(a, b, (((0,), (0,)), ((), ())), preferred_element_type=F32, precision=prec)


def _bf(x):
    return x.astype(BF16)


def _sigmoid(x):
    return 1.0 / (1.0 + jnp.exp(-x))


def _log_sigmoid(x):
    return jnp.minimum(x, 0.0) - jnp.log(1.0 + jnp.exp(-jnp.abs(x)))


def _gelu_tanh(x):
    return 0.5 * x * (1.0 + jnp.tanh(math.sqrt(2.0 / math.pi) * (x + 0.044715 * (x * x * x))))


def _iota(shape, dim):
    return lax.broadcasted_iota(jnp.int32, shape, dim)


def _idiv(x, d):
    assert d & (d - 1) == 0
    return jnp.right_shift(x, d.bit_length() - 1)


def _shift_rows(x, s, top):
    xr = pltpu.roll(x, s, 0)
    tr = pltpu.roll(top, s, 0)
    row = _iota((8, x.shape[1]), 0)
    head = jnp.where(row < s, tr, xr[:8])
    return jnp.concatenate([head, xr[8:]], axis=0)


def _norm_mm_kernel(x_ref, g_ref, w_ref, o_ref, xn_ref):
    @pl.when(pl.program_id(1) == 0)
    def _():
        x = x_ref[...]
        ms = jnp.mean(x * x, axis=-1, keepdims=True)
        xn_ref[...] = _bf(x * lax.rsqrt(ms + 1e-6) * g_ref[...])

    o_ref[...] = _dot(xn_ref[...], w_ref[...]).astype(o_ref.dtype)


def norm_matmul(x, gain, w, layer, tm, tn):
    m, k = x.shape
    n = w.shape[2]
    return pl.pallas_call(
        _norm_mm_kernel,
        grid=(m // tm, n // tn),
        in_specs=[pl.BlockSpec((tm, k), lambda i, j: (i, 0)),
                  pl.BlockSpec((None, 1, k), lambda i, j: (layer, 0, 0)),
                  pl.BlockSpec((None, k, tn), lambda i, j: (layer, 0, j))],
        out_specs=pl.BlockSpec((tm, tn), lambda i, j: (i, j)),
        out_shape=jax.ShapeDtypeStruct((m, n), F32),
        scratch_shapes=[pltpu.VMEM((tm, k), BF16)],
        compiler_params=_cparams("parallel", "arbitrary"),
        name="norm_matmul",
    )(x, gain, w)


def _mm_res_kernel(a_ref, w_ref, r_ref, o_ref):
    o_ref[...] = r_ref[...] + _dot(a_ref[...], w_ref[...])


def matmul_residual(a, w, res, layer, tm, tn):
    m, k = a.shape
    n = w.shape[2]
    return pl.pallas_call(
        _mm_res_kernel,
        grid=(m // tm, n // tn),
        in_specs=[pl.BlockSpec((tm, k), lambda i, j: (i, 0)),
                  pl.BlockSpec((None, k, tn), lambda i, j: (layer, 0, j)),
                  pl.BlockSpec((tm, tn), lambda i, j: (i, j))],
        out_specs=pl.BlockSpec((tm, tn), lambda i, j: (i, j)),
        out_shape=jax.ShapeDtypeStruct((m, n), F32),
        compiler_params=_cparams("parallel", "arbitrary"),
        name="matmul_residual",
    )(a, w, res)


def _merge_kernel(ya_ref, yb_ref, yc_ref, yd_ref, wb_ref, g0_ref, g1_ref, g2_ref, g3_ref, o_ref):
    acc = None
    for n, (y_ref, g_ref) in enumerate(((ya_ref, g0_ref), (yb_ref, g1_ref), (yc_ref, g2_ref), (yd_ref, g3_ref))):
        t = _sigmoid(g_ref[...]) * _dot(y_ref[...], wb_ref[n])
        acc = t if acc is None else acc + t
    o_ref[...] = _bf(acc)


def merge_branches(ys, w_branch, proj, layer, tm, tn):
    m = ys[0].shape[0]
    nj = D_MODEL // tn
    y_spec = pl.BlockSpec((tm, BRANCH_WIDTH), lambda i, j: (i, 0))
    gate_specs = [pl.BlockSpec((tm, tn), functools.partial(lambda i, j, n: (i, (P_GATES + n * D_MODEL) // tn + j), n=n))
                  for n in range(N_BRANCH)]
    return pl.pallas_call(
        _merge_kernel,
        grid=(m // tm, nj),
        in_specs=[y_spec] * 4 + [pl.BlockSpec((None, N_BRANCH, BRANCH_WIDTH, tn), lambda i, j: (layer, 0, 0, j))]
        + gate_specs,
        out_specs=pl.BlockSpec((tm, tn), lambda i, j: (i, j)),
        out_shape=jax.ShapeDtypeStruct((m, D_MODEL), BF16),
        compiler_params=_cparams("parallel", "arbitrary"),
        name="merge_branches",
    )(*ys, w_branch, proj, proj, proj, proj)


def _ffn_up_kernel(x_ref, g_ref, wg_ref, wv_ref, cw_g_ref, cw_v_ref, cb_g_ref, cb_v_ref, o_ref,
                   xn_ref, carry_ref, *, tiles_per_seq):
    i, j = pl.program_id(0), pl.program_id(1)

    @pl.when(j == 0)
    def _():
        x = x_ref[...]
        ms = jnp.mean(x * x, axis=-1, keepdims=True)
        xn_ref[...] = _bf(x * lax.rsqrt(ms + 1e-6) * g_ref[...])

    first = (i % tiles_per_seq) == 0

    @pl.when(first)
    def _():
        carry_ref[j] = jnp.zeros(carry_ref.shape[1:], F32)

    xn = xn_ref[...]
    outs = []
    for half, (w_ref, cw_ref, cb_ref) in enumerate(((wg_ref, cw_g_ref, cb_g_ref), (wv_ref, cw_v_ref, cb_v_ref))):
        u = _dot(xn, w_ref[...])
        top = carry_ref[j, half]
        carry_ref[j, half] = u[u.shape[0] - 8:]
        cw = cw_ref[...]
        outs.append(cw[0:1] * _shift_rows(u, 2, top) + cw[1:2] * _shift_rows(u, 1, top) + cw[2:3] * u + cb_ref[...])
    gate, val = outs
    o_ref[...] = _bf(gate * _sigmoid(gate) * val)


def ffn_up_conv(x, gain, w_up, conv_w, conv_b, layer, seq, tm, tn):
    m, k = x.shape
    nj = D_FF // tn
    kern = functools.partial(_ffn_up_kernel, tiles_per_seq=seq // tm)
    return pl.pallas_call(
        kern,
        grid=(m // tm, nj),
        in_specs=[pl.BlockSpec((tm, k), lambda i, j: (i, 0)),
                  pl.BlockSpec((None, 1, k), lambda i, j: (layer, 0, 0)),
                  pl.BlockSpec((None, k, tn), lambda i, j: (layer, 0, j)),
                  pl.BlockSpec((None, k, tn), lambda i, j: (layer, 0, nj + j)),
                  pl.BlockSpec((None, 8, tn), lambda i, j: (layer, 0, j)),
                  pl.BlockSpec((None, 8, tn), lambda i, j: (layer, 0, nj + j)),
                  pl.BlockSpec((None, 1, tn), lambda i, j: (layer, 0, j)),
                  pl.BlockSpec((None, 1, tn), lambda i, j: (layer, 0, nj + j))],
        out_specs=pl.BlockSpec((tm, tn), lambda i, j: (i, j)),
        out_shape=jax.ShapeDtypeStruct((m, D_FF), BF16),
        scratch_shapes=[pltpu.VMEM((tm, k), BF16), pltpu.VMEM((nj, 2, 8, tn), F32)],
        compiler_params=_cparams("arbitrary", "arbitrary"),
        name="ffn_up",
    )(x, gain, w_up, w_up, conv_w, conv_w, conv_b, conv_b)


def _gla_kernel(q_ref, k_ref, v_ref, g_ref, lr_ref, wgk_ref, bgk_ref, gain_ref, o_ref, st_ref):
    c = CHUNK

    @pl.when(pl.program_id(1) == 0)
    def _():
        st_ref[...] = jnp.zeros_like(st_ref)

    row = _iota((c, c), 0)
    col = _iota((c, c), 1)
    tri_incl = (col <= row).astype(F32)
    log_a = _log_sigmoid(_dot(lr_ref[...], wgk_ref[...], HI) + bgk_ref[...]) * (1.0 / GLA_NORMALIZER)
    b = _dot(tri_incl, log_a, HI)
    lane = _iota((1, LANE), 1)
    bd = _idiv(_iota((2 * GLA_DV, LANE), 0), GLA_DV) == _idiv(_iota((2 * GLA_DV, LANE), 1), GLA_DK)
    sub = 16
    o_heads = [None] * GLA_HEADS
    for p in range(GLA_HEADS // 2):
        sl = slice(p * LANE, (p + 1) * LANE)
        qp = q_ref[:, sl] * (GLA_DK ** -0.5)
        kp = k_ref[:, sl]
        bp = b[:, sl]
        b_last = bp[c - 1:c]
        st = st_ref[p]
        o_inter = _dot_nt(_bf(qp * jnp.exp(bp)), _bf(st))
        intra = [[], []]
        for i in range(c // sub):
            rows = slice(i * sub, (i + 1) * sub)
            nrow = (i + 1) * sub
            bref = bp[i * sub - 1:i * sub] if i > 0 else jnp.zeros((1, LANE), F32)
            qi = qp[rows] * jnp.exp(bp[rows] - bref)
            ki = _bf(kp[:nrow] * jnp.exp(jnp.minimum(bref - bp[:nrow], 80.0)))
            causal = _iota((sub, nrow), 1) <= (_iota((sub, nrow), 0) + i * sub)
            for hh in range(2):
                head_lanes = _idiv(lane, GLA_DK) == hh
                s = _dot_nt(_bf(jnp.where(head_lanes, qi, 0.0)), ki)
                s = jnp.where(causal, s, 0.0)
                h = 2 * p + hh
                intra[hh].append(_dot(_bf(s), _bf(v_ref[:nrow, h * GLA_DV:(h + 1) * GLA_DV])))
        for hh in range(2):
            o_heads[2 * p + hh] = o_inter[:, hh * GLA_DV:(hh + 1) * GLA_DV] + jnp.concatenate(intra[hh], axis=0)
        khat = kp * jnp.exp(b_last - bp)
        v_pair = v_ref[:, 2 * p * GLA_DV:(2 * p + 2) * GLA_DV]
        upd = _dot_tn(_bf(v_pair), _bf(khat))
        st_ref[p] = st * jnp.exp(b_last) + jnp.where(bd, upd, 0.0)
    gain = gain_ref[...]
    for h in range(GLA_HEADS):
        o = o_heads[h]
        y = o * lax.rsqrt(jnp.mean(o * o, axis=-1, keepdims=True) + 1e-6) * gain
        g = g_ref[:, h * GLA_DV:(h + 1) * GLA_DV]
        o_ref[:, h * GLA_DV:(h + 1) * GLA_DV] = _bf(y * (g * _sigmoid(g)))


def gla_mixer(proj, w_gk, b_gk, out_gain, layer, batch, seq):
    nt = seq // CHUNK
    def col(off, width):
        return pl.BlockSpec((CHUNK, width), lambda b, t: (b * nt + t, off // width))
    return pl.pallas_call(
        _gla_kernel,
        grid=(batch, nt),
        in_specs=[col(P_GLA_Q, 256), col(P_GLA_K, 256), col(P_GLA_V, 512), col(P_GLA_G, 512), col(P_GLA_LR, 128),
                  pl.BlockSpec((None, LANE, 256), lambda b, t: (layer, 0, 0)),
                  pl.BlockSpec((None, 1, 256), lambda b, t: (layer, 0, 0)),
                  pl.BlockSpec((None, 1, GLA_DV), lambda b, t: (layer, 0, 0))],
        out_specs=pl.BlockSpec((CHUNK, 512), lambda b, t: (b * nt + t, 0)),
        out_shape=jax.ShapeDtypeStruct((batch * seq, 512), BF16),
        scratch_shapes=[pltpu.VMEM((GLA_HEADS // 2, 2 * GLA_DV, LANE), F32)],
        compiler_params=_cparams("parallel", "arbitrary"),
        name="gla_mixer",
    )(proj, proj, proj, proj, proj, w_gk, b_gk, out_gain)


def _lru_kernel(x_ref, gb_ref, cw_ref, cb_ref, wa_ref, ba_ref, wi_ref, bi_ref, lam_ref, o_ref, xc_ref, h_ref):
    t = x_ref.shape[0]

    @pl.when(pl.program_id(1) == 0)
    def _():
        xc_ref[...] = jnp.zeros_like(xc_ref)
        h_ref[...] = jnp.zeros_like(h_ref)

    x = x_ref[...]
    top = xc_ref[...]
    xc_ref[...] = x[t - 8:]
    cw = cw_ref[...]
    xc = (cw[0:1] * _shift_rows(x, 3, top) + cw[1:2] * _shift_rows(x, 2, top) + cw[2:3] * _shift_rows(x, 1, top)
          + cw[3:4] * x + cb_ref[...])
    xcb = _bf(xc)
    ra, ri = [], []
    for n in range(LRU_BLOCKS):
        blk = xcb[:, n * LANE:(n + 1) * LANE]
        ra.append(_dot(blk, wa_ref[n]))
        ri.append(_dot(blk, wi_ref[n]))
    r = _sigmoid(jnp.concatenate(ra, axis=1) + ba_ref[...])
    gi = _sigmoid(jnp.concatenate(ri, axis=1) + bi_ref[...])
    log_a = LRU_C * r * _log_sigmoid(lam_ref[...])
    a = jnp.exp(log_a)
    u = jnp.sqrt(-jnp.tanh(log_a) * (a * a + 1.0)) * (gi * xc)
    row = _iota((t, LRU_WIDTH), 0)
    k = 1
    while k < t:
        a_sh = jnp.where(row < k, 1.0, pltpu.roll(a, k, 0))
        u_sh = jnp.where(row < k, 0.0, pltpu.roll(u, k, 0))
        u = u + a * u_sh
        a = a * a_sh
        k *= 2
    h = a * h_ref[0:1] + u
    h_ref[...] = jnp.broadcast_to(h[t - 1:t], h_ref.shape)
    o_ref[...] = _bf(h * _gelu_tanh(gb_ref[...]))


def lru_mixer(proj, conv_w, conv_b, w_a, b_a, w_i, b_i, lam, layer, batch, seq, tt):
    nt = seq // tt
    vec = pl.BlockSpec((None, 1, LRU_WIDTH), lambda b, t: (layer, 0, 0))
    wblk = pl.BlockSpec((None, LRU_BLOCKS, LANE, LANE), lambda b, t: (layer, 0, 0, 0))
    return pl.pallas_call(
        _lru_kernel,
        grid=(batch, nt),
        in_specs=[pl.BlockSpec((tt, 512), lambda b, t: (b * nt + t, P_LRU_X // 512)),
                  pl.BlockSpec((tt, 512), lambda b, t: (b * nt + t, P_LRU_G // 512)),
                  pl.BlockSpec((None, 8, LRU_WIDTH), lambda b, t: (layer, 0, 0)), vec, wblk, vec, wblk, vec, vec],
        out_specs=pl.BlockSpec((tt, 512), lambda b, t: (b * nt + t, 0)),
        out_shape=jax.ShapeDtypeStruct((batch * seq, 512), BF16),
        scratch_shapes=[pltpu.VMEM((8, LRU_WIDTH), F32), pltpu.VMEM((8, LRU_WIDTH), F32)],
        compiler_params=_cparams("parallel", "arbitrary"),
        name="lru_mixer",
    )(proj, proj, conv_w, conv_b, w_a, b_a, w_i, b_i, lam)


def _rwkv_kernel(f_ref, mu_ref, w0_ref, wl_ref, a0_ref, al_ref, gl_ref, kk_ref, ka_ref, rk_ref, lnw_ref, lnb_ref,
                 o_ref, st_ref, prev_ref, *, prec):
    c = CHUNK
    n = RWKV_DH

    @pl.when(pl.program_id(1) == 0)
    def _():
        st_ref[...] = jnp.zeros_like(st_ref)
        prev_ref[...] = jnp.zeros_like(prev_ref)

    feat = f_ref[...]
    prev = _shift_rows(feat, 1, prev_ref[...])
    prev_ref[...] = feat[c - 8:]
    xm = feat + (prev - feat) * mu_ref[...]
    r = xm[:, RW_R:RW_R + 512]
    k = xm[:, RW_K:RW_K + 512]
    v = xm[:, RW_V:RW_V + 512]
    log_w = -math.exp(-0.5) * _sigmoid(w0_ref[...] + _dot(_bf(jnp.tanh(xm[:, RW_XW:RW_XW + LANE])), wl_ref[...]))
    a = _sigmoid(a0_ref[...] + _dot(_bf(xm[:, RW_XA:RW_XA + LANE]), al_ref[...]))
    g = _dot(_bf(_sigmoid(xm[:, RW_XG:RW_XG + 256])), gl_ref[...])

    lane = _iota((1, LANE), 1)
    head_ones = (_idiv(_iota((LANE, LANE), 0), n) == _idiv(_iota((LANE, LANE), 1), n)).astype(F32)
    row = _iota((c, c), 0)
    col = _iota((c, c), 1)
    tri_incl = col <= row
    tri_strict = col < row
    tri_incl_f = tri_incl.astype(F32)
    eye = (col == row).astype(F32)

    y_pairs = []
    for p in range(RWKV_HEADS // 2):
        sl = slice(p * LANE, (p + 1) * LANE)
        rp, kp, vp, ap, lw = r[:, sl], k[:, sl], v[:, sl], a[:, sl], log_w[:, sl]
        kk = kp * kk_ref[:, sl]
        ss = _dot(kk * kk, head_ones, HI)
        kk = kk / jnp.maximum(jnp.sqrt(ss), 1e-12)
        k2 = kp * (1.0 + (ap - 1.0) * ka_ref[:, sl])
        beta = kk * ap
        cum = _dot(tri_incl_f, lw, HI)
        cum_last = cum[c - 1:c]
        e_pos = jnp.exp(cum)
        e_neg = jnp.exp(-cum)
        rt = rp * e_pos
        kt = kk * jnp.exp(cum - lw)
        kb = k2 * e_neg
        bb = beta * e_neg
        st = st_ref[p]
        u_sum = None
        y_sum = None
        for hh in range(2):
            hm = _idiv(lane, n) == hh
            ktm = jnp.where(hm, kt, 0.0)
            rtm = jnp.where(hm, rt, 0.0)
            vm = jnp.where(hm, vp, 0.0)
            a_mat = jnp.where(tri_strict, _dot_nt(ktm, bb, prec), 0.0)
            b_mat = jnp.where(tri_strict, _dot_nt(ktm, kb, prec), 0.0)
            rr_mat = jnp.where(tri_incl, _dot_nt(rtm, kb, prec), 0.0)
            rb_mat = jnp.where(tri_incl, _dot_nt(rtm, bb, prec), 0.0)
            nmat = -a_mat
            tinv = eye + nmat
            pw = 1
            while 2 * pw < c:
                nmat = _dot(nmat, nmat, prec)
                tinv = tinv + _dot(tinv, nmat, prec)
                pw *= 2
            rhs = _dot_nt(ktm, st, prec) + _dot(b_mat, vm, prec)
            u_h = _dot(tinv, rhs, prec)
            y_h = _dot_nt(rtm, st, prec) + _dot(rr_mat, vm, prec) - _dot(rb_mat, u_h, prec)
            u_sum = u_h if u_sum is None else u_sum + u_h
            y_sum = y_h if y_sum is None else y_sum + y_h
        e_last = jnp.exp(cum_last)
        upd = _dot_tn(vp, kb * e_last, prec) - _dot_tn(u_sum, bb * e_last, prec)
        st_ref[p] = st * e_last + jnp.where(head_ones > 0.5, upd, 0.0)
        mean = _dot(y_sum, head_ones, HI) * (1.0 / n)
        yc = y_sum - mean
        var = _dot(yc * yc, head_ones, HI) * (1.0 / n)
        yn = yc * lax.rsqrt(var + 64e-5) * lnw_ref[:, sl] + lnb_ref[:, sl]
        bonus = _dot(rp * k2 * rk_ref[:, sl], head_ones, HI)
        y_pairs.append(yn + bonus * vp)
    o_ref[...] = _bf(jnp.concatenate(y_pairs, axis=1) * g)


def rwkv_mixer(proj, mu, w0, w_lora, a0, a_lora, g_lora, k_k, k_a, r_k, ln_w, ln_b, layer, batch, seq, prec):
    nt = seq // CHUNK
    vec = pl.BlockSpec((None, 1, RWKV_WIDTH), lambda b, t: (layer, 0, 0))
    kern = functools.partial(_rwkv_kernel, prec=prec)
    return pl.pallas_call(
        kern,
        grid=(batch, nt),
        in_specs=[pl.BlockSpec((CHUNK, RW_TOTAL), lambda b, t: (b * nt + t, P_RWKV // RW_TOTAL)),
                  pl.BlockSpec((None, 1, RW_TOTAL), lambda b, t: (layer, 0, 0)),
                  vec, pl.BlockSpec((None, LANE, RWKV_WIDTH), lambda b, t: (layer, 0, 0)),
                  vec, pl.BlockSpec((None, LANE, RWKV_WIDTH), lambda b, t: (layer, 0, 0)),
                  pl.BlockSpec((None, RWKV_G_LORA, RWKV_WIDTH), lambda b, t: (layer, 0, 0)),
                  vec, vec, vec, vec, vec],
        out_specs=pl.BlockSpec((CHUNK, 512), lambda b, t: (b * nt + t, 0)),
        out_shape=jax.ShapeDtypeStruct((batch * seq, 512), BF16),
        scratch_shapes=[pltpu.VMEM((RWKV_HEADS // 2, LANE, LANE), F32), pltpu.VMEM((8, RW_TOTAL), F32)],
        compiler_params=_cparams("parallel", "arbitrary"),
        name="rwkv_mixer",
    )(proj, mu, w0, w_lora, a0, a_lora, g_lora, k_k, k_a, r_k, ln_w, ln_b)


def _head_rms(x, gain):
    return x * lax.rsqrt(jnp.mean(x * x, axis=-1, keepdims=True) + 1e-6) * gain


def _nsa_prep_kernel(q_ref, kv_ref, qg_ref, kg_ref, qn_ref, ks_ref, vs_ref, kw_ref, vw_ref):
    qg = qg_ref[...] * (NSA_DH ** -0.5)
    for h in range(NSA_HEADS):
        sl = slice(h * NSA_DH, (h + 1) * NSA_DH)
        qn_ref[:, sl] = _bf(_head_rms(q_ref[:, sl], qg))
    kg = kg_ref[...]
    ks_ref[...] = _bf(_head_rms(kv_ref[:, 2 * NSA_DH:3 * NSA_DH], kg))
    vs_ref[...] = _bf(kv_ref[:, 3 * NSA_DH:4 * NSA_DH])
    kw_ref[...] = _bf(_head_rms(kv_ref[:, 4 * NSA_DH:5 * NSA_DH], kg))
    vw_ref[...] = _bf(kv_ref[:, 5 * NSA_DH:6 * NSA_DH])


def nsa_prep(proj, q_gain, k_gain, layer, tt):
    m = proj.shape[0]
    gain = pl.BlockSpec((None, 1, NSA_DH), lambda i: (layer, 0, 0))
    kv_out = pl.BlockSpec((tt, NSA_DH), lambda i: (i, 0))
    kv_shape = jax.ShapeDtypeStruct((m, NSA_DH), BF16)
    return pl.pallas_call(
        _nsa_prep_kernel,
        grid=(m // tt,),
        in_specs=[pl.BlockSpec((tt, 512), lambda i: (i, P_NSA_Q // 512)),
                  pl.BlockSpec((tt, 768), lambda i: (i, P_NSA_KV // 768)), gain, gain],
        out_specs=[pl.BlockSpec((tt, 512), lambda i: (i, 0)), kv_out, kv_out, kv_out, kv_out],
        out_shape=[jax.ShapeDtypeStruct((m, 512), BF16), kv_shape, kv_shape, kv_shape, kv_shape],
        compiler_params=_cparams("parallel"),
        name="nsa_prep",
    )(proj, proj, q_gain, k_gain)


def _nsa_compress_kernel(kg_ref, vg_ref, pos_ref, k1_ref, k2_ref, v1_ref, v2_ref, gain_ref, kc_ref, vc_ref):
    nc = kg_ref.shape[0]
    half = NSA_CMP_STRIDE * NSA_DH
    pos = _bf(pos_ref[...])

    def compress(g_ref, w1_ref, w2_ref):
        grp = _bf(g_ref[...])
        first = _dot(grp, w1_ref[:half])
        second = _dot(grp, w1_ref[half:])
        const = _dot(pos, w1_ref[...])[0:1]
        hid = first + pltpu.roll(second, nc - 1, 0) + const
        return _dot(_bf(_gelu_tanh(hid)), w2_ref[...])

    kc_ref[...] = _bf(_head_rms(compress(kg_ref, k1_ref, k2_ref), gain_ref[...]))
    vc_ref[...] = _bf(compress(vg_ref, v1_ref, v2_ref))


def nsa_compress(kgrp, vgrp, pos, k1, k2, v1, v2, k_gain, layer):
    batch, nc, width = kgrp.shape
    grp = pl.BlockSpec((None, nc, width), lambda b: (b, 0, 0))
    w1 = pl.BlockSpec((None, 2 * width, NSA_DH), lambda b: (layer, 0, 0))
    w2 = pl.BlockSpec((None, NSA_DH, NSA_DH), lambda b: (layer, 0, 0))
    out = pl.BlockSpec((None, nc, NSA_DH), lambda b: (b, 0, 0))
    shape = jax.ShapeDtypeStruct((batch, nc, NSA_DH), BF16)
    return pl.pallas_call(
        _nsa_compress_kernel,
        grid=(batch,),
        in_specs=[grp, grp, pl.BlockSpec((None, 8, 2 * width), lambda b: (layer, 0, 0)), w1, w2, w1, w2,
                  pl.BlockSpec((None, 1, NSA_DH), lambda b: (layer, 0, 0))],
        out_specs=[out, out],
        out_shape=[shape, shape],
        compiler_params=_cparams("parallel"),
        name="nsa_compress",
    )(kgrp, vgrp, pos, k1, k2, v1, v2, k_gain)


def _nsa_cmp_kernel(cfar_ref, q_ref, kc_ref, vc_ref, band_ref, ovl_ref, gate_ref, o_ref, sel_ref):
    qb = NSA_QBLOCK
    nc = kc_ref.shape[0]
    i = pl.program_id(1)
    kc = kc_ref[...]
    vc = vc_ref[...]
    r = _iota((qb, nc), 0)
    ncol = _iota((qb, nc), 1)
    dist = qb * i + r - NSA_CMP_STRIDE * ncol - (NSA_CMP_LEN - 1)
    visible = dist >= 0
    lo = (qb // NSA_CMP_STRIDE) * i - 9
    in_band = (ncol >= lo) & (ncol <= lo + 15)
    gate = gate_ref[...]
    p_sum = jnp.zeros((qb, nc), F32)
    for h in range(NSA_HEADS):
        sl = slice(h * NSA_DH, (h + 1) * NSA_DH)
        band = jnp.concatenate([band_ref[h]] * (nc // LANE), axis=1)
        logit = _dot_nt(q_ref[:, sl], kc) + jnp.where(in_band, band, cfar_ref[h])
        logit = jnp.where(visible, logit, MASKED)
        mx = jnp.max(logit, axis=-1, keepdims=True)
        p = jnp.where(visible, jnp.exp(logit - mx), 0.0)
        p = p / jnp.maximum(jnp.sum(p, axis=-1, keepdims=True), 1e-30)
        p_sum = p_sum + p
        o_ref[:, sl] = _sigmoid(gate[:, h:h + 1]) * _dot(_bf(p), vc)
    p_hi = _bf(p_sum)
    p_lo = _bf(p_sum - p_hi.astype(F32))
    ovl = ovl_ref[...]
    score = _dot(p_hi, ovl) + _dot(p_lo, ovl)
    blk = _iota((qb, LANE), 1)
    pos = qb * i + _iota((qb, LANE), 0)
    cur = _idiv(pos, NSA_SEL_BLOCK)
    forced = (blk == 0) | (blk == cur) | (blk == cur - 1)
    work = jnp.where(forced, NEG_BIG, jnp.where(blk * NSA_SEL_BLOCK <= pos, score, -NEG_BIG))
    sel = jnp.zeros((qb, LANE), F32)
    blk_f = blk.astype(F32)
    for _ in range(NSA_SEL_TOPK):
        mx = jnp.max(work, axis=-1, keepdims=True)
        first = jnp.min(jnp.where(work == mx, blk_f, float(LANE)), axis=-1, keepdims=True)
        pick = blk_f == first
        sel = jnp.where(pick, 1.0, sel)
        work = jnp.where(pick, -jnp.inf, work)
    sel_ref[...] = _bf(sel)


def nsa_cmp_attention(cfar, qn, kc, vc, band, ovl, proj, batch, seq):
    nq = seq // NSA_QBLOCK
    nc = kc.shape[1]
    full = pl.BlockSpec((None, nc, NSA_DH), lambda b, i: (b, 0, 0))
    return pl.pallas_call(
        _nsa_cmp_kernel,
        grid=(batch, nq),
        in_specs=[pl.BlockSpec(memory_space=pltpu.SMEM),
                  pl.BlockSpec((NSA_QBLOCK, 512), lambda b, i: (b * nq + i, 0)), full, full,
                  pl.BlockSpec((None, NSA_HEADS, NSA_QBLOCK, LANE), lambda b, i: (i % 16, 0, 0, 0)),
                  pl.BlockSpec((nc, LANE), lambda b, i: (0, 0)),
                  pl.BlockSpec((NSA_QBLOCK, LANE), lambda b, i: (b * nq + i, P_NSA_GATE // LANE))],
        out_specs=[pl.BlockSpec((NSA_QBLOCK, 512), lambda b, i: (b * nq + i, 0)),
                   pl.BlockSpec((NSA_QBLOCK, LANE), lambda b, i: (b * nq + i, 0))],
        out_shape=[jax.ShapeDtypeStruct((batch * seq, 512), F32), jax.ShapeDtypeStruct((batch * seq, LANE), BF16)],
        compiler_params=_cparams("parallel", "arbitrary"),
        name="nsa_cmp_attention",
    )(cfar, qn, kc, vc, band, ovl, proj)


def _nsa_sel_win_kernel(cfar_ref, q_ref, ks_ref, vs_ref, kw_ref, vw_ref, sel_ref, toep_ref, gate_ref, ocmp_ref,
                        o_ref, m_ref, l_ref, acc_ref):
    qb = NSA_QBLOCK
    nh = NSA_HEADS
    i = pl.program_id(1)
    qs = jnp.concatenate([q_ref[:, h * NSA_DH:(h + 1) * NSA_DH] for h in range(nh)], axis=0)
    r = jnp.bitwise_and(_iota((nh * qb, qb), 0), qb - 1)
    lcol = _iota((nh * qb, qb), 1)
    bias_diag = jnp.concatenate([toep_ref[h, 0] for h in range(nh)], axis=0)
    bias_prev = jnp.concatenate([toep_ref[h, 1] for h in range(nh)], axis=0)
    bias_far = jnp.concatenate([jnp.full((qb, qb), cfar_ref[h], F32) for h in range(nh)], axis=0)

    def reset():
        m_ref[...] = jnp.full_like(m_ref, MASKED)
        l_ref[...] = jnp.zeros_like(l_ref)
        acc_ref[...] = jnp.zeros_like(acc_ref)

    def attend(k_tile, v_tile, bias, valid):
        s = jnp.where(valid, _dot_nt(qs, k_tile) + bias, MASKED)
        m_old = m_ref[...]
        m_new = jnp.maximum(m_old, jnp.max(s, axis=-1, keepdims=True))
        p = jnp.where(valid, jnp.exp(s - m_new), 0.0)
        alpha = jnp.exp(m_old - m_new)
        l_ref[...] = alpha * l_ref[...] + jnp.sum(p, axis=-1, keepdims=True)
        acc_ref[...] = alpha * acc_ref[...] + _dot(_bf(p), v_tile)
        m_ref[...] = m_new

    def result():
        return acc_ref[...] / jnp.maximum(l_ref[...], 1e-30)

    reset()
    sel = sel_ref[...]
    blk_row = _iota((LANE, qb), 0)
    key_blk = _idiv(_iota((LANE, qb), 1), NSA_SEL_BLOCK)

    def sel_body(kt, carry):
        start = pl.multiple_of(kt * qb, qb)
        expand = _bf((blk_row == key_blk + kt * (qb // NSA_SEL_BLOCK)).astype(F32))
        chosen = _dot(sel, expand)
        chosen = jnp.concatenate([chosen] * nh, axis=0) > 0.5
        bias = jnp.where(kt == i, bias_diag, jnp.where(kt == i - 1, bias_prev, bias_far))
        valid = chosen & ((kt < i) | (lcol <= r))
        attend(ks_ref[pl.ds(start, qb), :], vs_ref[pl.ds(start, qb), :], bias, valid)
        return carry

    lax.fori_loop(0, i + 1, sel_body, 0)
    o_sel = result()

    reset()
    nw = NSA_WINDOW // qb
    for d in range(nw + 1):
        jt = i - nw + d
        start = pl.multiple_of(jnp.maximum(jt, 0) * qb, qb)
        if d == nw:
            bias, valid = bias_diag, lcol <= r
        elif d == nw - 1:
            bias, valid = bias_prev, lcol >= 0
        elif d == 0:
            bias, valid = bias_far, r < lcol
        else:
            bias, valid = bias_far, lcol >= 0
        valid = valid & (jt >= 0)
        attend(kw_ref[pl.ds(start, qb), :], vw_ref[pl.ds(start, qb), :], bias, valid)
    o_win = result()

    gate = gate_ref[...]
    for h in range(nh):
        rows = slice(h * qb, (h + 1) * qb)
        sl = slice(h * NSA_DH, (h + 1) * NSA_DH)
        g_sel = _sigmoid(gate[:, nh + h:nh + h + 1])
        g_win = _sigmoid(gate[:, 2 * nh + h:2 * nh + h + 1])
        o_ref[:, sl] = _bf(ocmp_ref[:, sl] + g_sel * o_sel[rows] + g_win * o_win[rows])


def nsa_sel_win_attention(cfar, qn, ks, vs, kw, vw, sel, toep, proj, ocmp, batch, seq):
    nq = seq // NSA_QBLOCK
    full = pl.BlockSpec((None, seq, NSA_DH), lambda b, i: (b, 0, 0))
    rows = lambda width, cb=0: pl.BlockSpec((NSA_QBLOCK, width), lambda b, i: (b * nq + i, cb))
    return pl.pallas_call(
        _nsa_sel_win_kernel,
        grid=(batch, nq),
        in_specs=[pl.BlockSpec(memory_space=pltpu.SMEM), rows(512), full, full, full, full, rows(LANE),
                  pl.BlockSpec((NSA_HEADS, 2, NSA_QBLOCK, NSA_QBLOCK), lambda b, i: (0, 0, 0, 0)),
                  rows(LANE, P_NSA_GATE // LANE), rows(512)],
        out_specs=rows(512),
        out_shape=jax.ShapeDtypeStruct((batch * seq, 512), BF16),
        scratch_shapes=[pltpu.VMEM((NSA_HEADS * NSA_QBLOCK, 1), F32), pltpu.VMEM((NSA_HEADS * NSA_QBLOCK, 1), F32),
                        pltpu.VMEM((NSA_HEADS * NSA_QBLOCK, NSA_DH), F32)],
        compiler_params=_cparams("parallel", "arbitrary"),
        name="nsa_sel_win_attention",
    )(cfar, qn, ks, vs, kw, vw, sel, toep, proj, ocmp)


def _t5_bucket_np(dist):
    n = np.maximum(dist, 0)
    max_exact = NUM_BUCKETS // 2
    nf = np.maximum(n, 1).astype(np.float64)
    large = max_exact + (np.log(nf / max_exact) / math.log(MAX_DISTANCE / max_exact)
                         * (NUM_BUCKETS - max_exact)).astype(np.int64)
    large = np.minimum(large, NUM_BUCKETS - 1)
    return np.where(n < max_exact, n, large).astype(np.int32)


def _bias_tables(rel_bias):
    qb = NSA_QBLOCK
    r = np.arange(qb)[:, None]
    l = np.arange(qb)[None, :]
    toep_idx = np.stack([_t5_bucket_np(r - l), _t5_bucket_np(qb + r - l)])
    toep = jnp.transpose(rel_bias[toep_idx], (3, 0, 1, 2))
    per = qb // NSA_CMP_STRIDE
    band_idx = np.zeros((16, qb, LANE), np.int32)
    for im in range(16):
        base = per * im - 9
        n = base + ((np.arange(LANE) - base) % LANE)
        dist = qb * im + r - NSA_CMP_STRIDE * n[None, :] - (NSA_CMP_LEN - 1)
        band_idx[im] = _t5_bucket_np(dist)
    band = jnp.transpose(rel_bias[band_idx], (0, 3, 1, 2))
    cfar = rel_bias[NUM_BUCKETS - 1]
    return toep.astype(F32), band.astype(F32), cfar.astype(F32)


def _overlap_table(nc, seq):
    n_cmp = (seq - NSA_CMP_LEN) // NSA_CMP_STRIDE + 1
    n_sel = seq // NSA_SEL_BLOCK
    cs = np.arange(nc)[:, None] * NSA_CMP_STRIDE
    ss = np.arange(LANE)[None, :] * NSA_SEL_BLOCK
    ovl = (cs < ss + NSA_SEL_BLOCK) & (cs + NSA_CMP_LEN > ss)
    ovl &= (np.arange(nc)[:, None] < n_cmp) & (np.arange(LANE)[None, :] < n_sel)
    return jnp.asarray(ovl.astype(np.float32), dtype=BF16)


def _pad_axis(t, axis, size):
    pad = [(0, 0)] * t.ndim
    pad[axis] = (0, size - t.shape[axis])
    return jnp.pad(t, pad)


def _pack_w_in(w_in):
    nl, k, _ = w_in.shape
    src = {}
    start = 0
    names = ("gla_q", "gla_k", "gla_v", "gla_g", "gla_lr", "lru_x", "lru_g", "nsa_q", "nsa_kv", "nsa_gate", "rwkv",
             "gates")
    widths = (256, 256, 512, 512, GLA_LOWRANK, 512, 512, 512, 768, 12, 1984, N_BRANCH * D_MODEL)
    for name, wd in zip(names, widths):
        src[name] = (start, wd)
        start += wd
    def piece(name, width):
        s, wd = src[name]
        return _pad_axis(w_in[:, :, s:s + wd], 2, width)
    rs, _ = src["rwkv"]
    rw = jnp.concatenate([
        w_in[:, :, rs:rs + 1536],
        _pad_axis(w_in[:, :, rs + 1536:rs + 1632], 2, LANE),
        _pad_axis(w_in[:, :, rs + 1632:rs + 1728], 2, LANE),
        w_in[:, :, rs + 1728:rs + 1984]], axis=2)
    packed = jnp.concatenate([
        piece("gates", 8192), rw, piece("lru_x", 512), piece("lru_g", 512), piece("nsa_q", 512), piece("gla_v", 512),
        piece("gla_g", 512), piece("gla_q", 256), piece("nsa_kv", 768), piece("gla_k", 256), piece("gla_lr", LANE),
        piece("nsa_gate", LANE)], axis=2)
    assert packed.shape[2] == P_TOTAL
    return packed.astype(BF16)


def _pack_rwkv_vec(t):
    return jnp.concatenate([t[:, :1536], _pad_axis(t[:, 1536:1632], 1, LANE), _pad_axis(t[:, 1632:1728], 1, LANE),
                            t[:, 1728:1984]], axis=1)[:, None, :]


def _row(t):
    return t[:, None, :]


RWKV_PREC = HI


def kernel(x, rel_bias, attn_norm, ffn_norm, w_in, gla_w_gk, gla_b_gk, gla_out_norm, lru_conv_w, lru_conv_b, lru_w_a, lru_b_a, lru_w_i, lru_b_i, lru_lambda, nsa_cmp_pos, nsa_cmp_k1, nsa_cmp_k2, nsa_cmp_v1, nsa_cmp_v2, nsa_q_norm, nsa_k_norm, rwkv_mu, rwkv_w0, rwkv_w_lora, rwkv_a0, rwkv_a_lora, rwkv_g_lora, rwkv_k_k, rwkv_k_a, rwkv_r_k, rwkv_ln_w, rwkv_ln_b, w_branch, w_out, ffn_up, ffn_conv_w, ffn_conv_b, ffn_down):
    batch, seq, d = x.shape
    depth = w_in.shape[0]
    m = batch * seq
    nc = seq // NSA_CMP_STRIDE
    assert d == D_MODEL and seq % 2048 == 0

    w_in_p = _pack_w_in(w_in)
    w_branch_b = _bf(w_branch)
    w_out_b = _bf(w_out)
    ffn_up_b = _bf(ffn_up)
    ffn_down_b = _bf(ffn_down)
    gla_w_gk_p = _pad_axis(gla_w_gk, 1, LANE)
    lru_conv_w_p = _pad_axis(lru_conv_w, 1, 8)
    ffn_conv_w_p = _pad_axis(ffn_conv_w, 1, 8)
    rwkv_w_lora_p = _bf(_pad_axis(rwkv_w_lora, 1, LANE))
    rwkv_a_lora_p = _bf(_pad_axis(rwkv_a_lora, 1, LANE))
    cmp_pos_p = jnp.broadcast_to(nsa_cmp_pos.reshape(depth, 1, NSA_CMP_LEN * NSA_DH), (depth, 8, NSA_CMP_LEN * NSA_DH))
    toep, band, cfar = _bias_tables(rel_bias)
    ovl = _overlap_table(nc, seq)

    tm = min(1024, seq)
    xf = x.reshape(m, d)
    for l in range(depth):
        proj = norm_matmul(xf, _row(attn_norm), w_in_p, l, tm, 512)
        y_a = gla_mixer(proj, gla_w_gk_p, _row(gla_b_gk), _row(gla_out_norm), l, batch, seq)
        y_b = lru_mixer(proj, lru_conv_w_p, _row(lru_conv_b), _bf(lru_w_a), _row(lru_b_a), _bf(lru_w_i), _row(lru_b_i),
                        _row(lru_lambda), l, batch, seq, min(256, seq))
        qn, ks, vs, kw, vw = nsa_prep(proj, _row(nsa_q_norm), _row(nsa_k_norm), l, min(512, seq))
        kgrp = proj[:, P_NSA_KV:P_NSA_KV + NSA_DH].reshape(batch, nc, NSA_CMP_STRIDE * NSA_DH)
        vgrp = proj[:, P_NSA_KV + NSA_DH:P_NSA_KV + 2 * NSA_DH].reshape(batch, nc, NSA_CMP_STRIDE * NSA_DH)
        kc, vc = nsa_compress(kgrp, vgrp, cmp_pos_p, _bf(nsa_cmp_k1), _bf(nsa_cmp_k2), _bf(nsa_cmp_v1),
                              _bf(nsa_cmp_v2), _row(nsa_k_norm), l)
        ocmp, sel = nsa_cmp_attention(cfar, qn, kc, vc, band, ovl, proj, batch, seq)
        seq3 = lambda t: t.reshape(batch, seq, NSA_DH)
        y_c = nsa_sel_win_attention(cfar, qn, seq3(ks), seq3(vs), seq3(kw), seq3(vw), sel, toep, proj, ocmp, batch, seq)
        y_d = rwkv_mixer(proj, _pack_rwkv_vec(rwkv_mu), _row(rwkv_w0), rwkv_w_lora_p, _row(rwkv_a0), rwkv_a_lora_p,
                         _bf(rwkv_g_lora), _row(rwkv_k_k), _row(rwkv_k_a), _row(rwkv_r_k.reshape(depth, RWKV_WIDTH)),
                         _row(rwkv_ln_w), _row(rwkv_ln_b), l, batch, seq, RWKV_PREC)
        merged = merge_branches((y_a, y_b, y_c, y_d), w_branch_b, proj, l, min(512, seq), 512)
        xf = matmul_residual(merged, w_out_b, xf, l, tm, 512)
        act = ffn_up_conv(xf, _row(ffn_norm), ffn_up_b, ffn_conv_w_p, _row(ffn_conv_b), l, seq, tm, 512)
        xf = matmul_residual(act, ffn_down_b, xf, l, tm, 512)
    return xf.reshape(batch, seq, d)
```

```python
import functools
import math

import numpy as np
import jax
import jax.numpy as jnp
from jax import lax
from jax.experimental import pallas as pl
from jax.experimental.pallas import tpu as pltpu

F32 = jnp.float32
BF16 = jnp.bfloat16
HI = lax.Precision.HIGHEST

LANE = 128
VMEM_LIMIT = 56 * 1024 * 1024

D_MODEL = 2048
N_BRANCH = 4
BRANCH_WIDTH = 512

GLA_HEADS, GLA_DK, GLA_DV, GLA_LOWRANK, GLA_NORMALIZER = 4, 64, 128, 16, 16.0
LRU_WIDTH, LRU_BLOCKS, LRU_CONV, LRU_C = 512, 4, 4, 8.0
NSA_HEADS, NSA_DH = 4, 128
NSA_CMP_LEN, NSA_CMP_STRIDE, NSA_SEL_BLOCK, NSA_SEL_TOPK, NSA_WINDOW, NSA_QBLOCK = 32, 16, 64, 16, 512, 128
RWKV_HEADS, RWKV_DH, RWKV_WIDTH = 8, 64, 512
RWKV_W_LORA, RWKV_A_LORA, RWKV_G_LORA = 96, 96, 256
NUM_BUCKETS, MAX_DISTANCE = 32, 128
D_FF, FFN_CONV = 5632, 3
NEG_BIG = 1e9
MASKED = -1e30
M_INIT = -1e20

P_GATES, P_RWKV, P_LRU_X, P_LRU_G, P_NSA_Q, P_GLA_V, P_GLA_G = 0, 8192, 10240, 10752, 11264, 11776, 12288
P_GLA_Q, P_NSA_KV, P_GLA_K, P_GLA_LR, P_NSA_GATE, P_TOTAL = 12800, 13056, 13824, 14080, 14208, 14336
RW_R, RW_K, RW_V, RW_XW, RW_XA, RW_XG, RW_TOTAL = 0, 512, 1024, 1536, 1664, 1792, 2048

CHUNK = 64


def _cparams(*sem):
    return pltpu.CompilerParams(dimension_semantics=sem, vmem_limit_bytes=VMEM_LIMIT)


def _dot(a, b, prec=None):
    return jnp.dot(a, b, preferred_element_type=F32, precision=prec)


def _dot_nt(a, b, prec=None):
    return lax.dot_general(a, b, (((1,), (1,)), ((), ())), preferred_element_type=F32, precision=prec)


def _dot_tn(a, b, prec=None):
    return lax.dot_general(a, b, (((0,), (0,)), ((), ())), preferred_element_type=F32, precision=prec)


def _bf(x):
    return x.astype(BF16)


def _sigmoid(x):
    return 1.0 / (1.0 + jnp.exp(-x))


def _log_sigmoid(x):
    return jnp.minimum(x, 0.0) - jnp.log(1.0 + jnp.exp(-jnp.abs(x)))


def _gelu_tanh(x):
    return 0.5 * x * (1.0 + jnp.tanh(math.sqrt(2.0 / math.pi) * (x + 0.044715 * (x * x * x))))


def _iota(shape, dim):
    return lax.broadcasted_iota(jnp.int32, shape, dim)


def _idiv(x, d):
    assert d & (d - 1) == 0
    return jnp.right_shift(x, d.bit_length() - 1)


def _shift_rows(x, s, top):
    xr = pltpu.roll(x, s, 0)
    tr = pltpu.roll(top, s, 0)
    row = _iota((8, x.shape[1]), 0)
    head = jnp.where(row < s, tr, xr[:8])
    return jnp.concatenate([head, xr[8:]], axis=0)


def _norm_mm_kernel(x_ref, g_ref, w_ref, o_ref, xn_ref):
    @pl.when(pl.program_id(1) == 0)
    def _():
        x = x_ref[...]
        ms = jnp.mean(x * x, axis=-1, keepdims=True)
        xn_ref[...] = _bf(x * lax.rsqrt(ms + 1e-6) * g_ref[...])

    o_ref[...] = _dot(xn_ref[...], w_ref[...]).astype(o_ref.dtype)


def norm_matmul(x, gain, w, layer, tm, tn):
    m, k = x.shape
    n = w.shape[2]
    return pl.pallas_call(
        _norm_mm_kernel,
        grid=(m // tm, n // tn),
        in_specs=[pl.BlockSpec((tm, k), lambda i, j: (i, 0)),
                  pl.BlockSpec((None, 1, k), lambda i, j: (layer, 0, 0)),
                  pl.BlockSpec((None, k, tn), lambda i, j: (layer, 0, j))],
        out_specs=pl.BlockSpec((tm, tn), lambda i, j: (i, j)),
        out_shape=jax.ShapeDtypeStruct((m, n), F32),
        scratch_shapes=[pltpu.VMEM((tm, k), BF16)],
        compiler_params=_cparams("parallel", "arbitrary"),
        name="norm_matmul",
    )(x, gain, w)


def _mm_res_kernel(a_ref, w_ref, r_ref, o_ref):
    o_ref[...] = r_ref[...] + _dot(a_ref[...], w_ref[...])


def matmul_residual(a, w, res, layer, tm, tn):
    m, k = a.shape
    n = w.shape[2]
    return pl.pallas_call(
        _mm_res_kernel,
        grid=(m // tm, n // tn),
        in_specs=[pl.BlockSpec((tm, k), lambda i, j: (i, 0)),
                  pl.BlockSpec((None, k, tn), lambda i, j: (layer, 0, j)),
                  pl.BlockSpec((tm, tn), lambda i, j: (i, j))],
        out_specs=pl.BlockSpec((tm, tn), lambda i, j: (i, j)),
        out_shape=jax.ShapeDtypeStruct((m, n), F32),
        compiler_params=_cparams("parallel", "arbitrary"),
        name="matmul_residual",
    )(a, w, res)


def _merge_kernel(ya_ref, yb_ref, yc_ref, yd_ref, wb_ref, g0_ref, g1_ref, g2_ref, g3_ref, o_ref):
    acc = None
    for n, (y_ref, g_ref) in enumerate(((ya_ref, g0_ref), (yb_ref, g1_ref), (yc_ref, g2_ref), (yd_ref, g3_ref))):
        t = _sigmoid(g_ref[...]) * _dot(y_ref[...], wb_ref[n])
        acc = t if acc is None else acc + t
    o_ref[...] = _bf(acc)


def merge_branches(ys, w_branch, proj, layer, tm, tn):
    m = ys[0].shape[0]
    nj = D_MODEL // tn
    y_spec = pl.BlockSpec((tm, BRANCH_WIDTH), lambda i, j: (i, 0))
    gate_specs = [pl.BlockSpec((tm, tn), functools.partial(lambda i, j, n: (i, (P_GATES + n * D_MODEL) // tn + j), n=n))
                  for n in range(N_BRANCH)]
    return pl.pallas_call(
        _merge_kernel,
        grid=(m // tm, nj),
        in_specs=[y_spec] * 4 + [pl.BlockSpec((None, N_BRANCH, BRANCH_WIDTH, tn), lambda i, j: (layer, 0, 0, j))]
        + gate_specs,
        out_specs=pl.BlockSpec((tm, tn), lambda i, j: (i, j)),
        out_shape=jax.ShapeDtypeStruct((m, D_MODEL), BF16),
        compiler_params=_cparams("parallel", "arbitrary"),
        name="merge_branches",
    )(*ys, w_branch, proj, proj, proj, proj)


def _ffn_up_kernel(x_ref, g_ref, wg_ref, wv_ref, cw_g_ref, cw_v_ref, cb_g_ref, cb_v_ref, o_ref,
                   xn_ref, carry_ref, *, tiles_per_seq):
    i, j = pl.program_id(0), pl.program_id(1)

    @pl.when(j == 0)
    def _():
        x = x_ref[...]
        ms = jnp.mean(x * x, axis=-1, keepdims=True)
        xn_ref[...] = _bf(x * lax.rsqrt(ms + 1e-6) * g_ref[...])

    first = (i % tiles_per_seq) == 0

    @pl.when(first)
    def _():
        carry_ref[j] = jnp.zeros(carry_ref.shape[1:], F32)

    xn = xn_ref[...]
    outs = []
    for half, (w_ref, cw_ref, cb_ref) in enumerate(((wg_ref, cw_g_ref, cb_g_ref), (wv_ref, cw_v_ref, cb_v_ref))):
        u = _dot(xn, w_ref[...])
        top = carry_ref[j, half]
        carry_ref[j, half] = u[u.shape[0] - 8:]
        cw = cw_ref[...]
        outs.append(cw[0:1] * _shift_rows(u, 2, top) + cw[1:2] * _shift_rows(u, 1, top) + cw[2:3] * u + cb_ref[...])
    gate, val = outs
    o_ref[...] = _bf(gate * _sigmoid(gate) * val)


def ffn_up_conv(x, gain, w_up, conv_w, conv_b, layer, seq, tm, tn):
    m, k = x.shape
    nj = D_FF // tn
    kern = functools.partial(_ffn_up_kernel, tiles_per_seq=seq // tm)
    return pl.pallas_call(
        kern,
        grid=(m // tm, nj),
        in_specs=[pl.BlockSpec((tm, k), lambda i, j: (i, 0)),
                  pl.BlockSpec((None, 1, k), lambda i, j: (layer, 0, 0)),
                  pl.BlockSpec((None, k, tn), lambda i, j: (layer, 0, j)),
                  pl.BlockSpec((None, k, tn), lambda i, j: (layer, 0, nj + j)),
                  pl.BlockSpec((None, 8, tn), lambda i, j: (layer, 0, j)),
                  pl.BlockSpec((None, 8, tn), lambda i, j: (layer, 0, nj + j)),
                  pl.BlockSpec((None, 1, tn), lambda i, j: (layer, 0, j)),
                  pl.BlockSpec((None, 1, tn), lambda i, j: (layer, 0, nj + j))],
        out_specs=pl.BlockSpec((tm, tn), lambda i, j: (i, j)),
        out_shape=jax.ShapeDtypeStruct((m, D_FF), BF16),
        scratch_shapes=[pltpu.VMEM((tm, k), BF16), pltpu.VMEM((nj, 2, 8, tn), F32)],
        compiler_params=_cparams("arbitrary", "arbitrary"),
        name="ffn_up",
    )(x, gain, w_up, w_up, conv_w, conv_w, conv_b, conv_b)


def _gla_kernel(q_ref, k_ref, v_ref, g_ref, lr_ref, wgk_ref, bgk_ref, gain_ref, o_ref, st_ref):
    c = CHUNK

    @pl.when(pl.program_id(1) == 0)
    def _():
        st_ref[...] = jnp.zeros_like(st_ref)

    row = _iota((c, c), 0)
    col = _iota((c, c), 1)
    tri_incl = (col <= row).astype(F32)
    log_a = _log_sigmoid(_dot(lr_ref[...], wgk_ref[...], HI) + bgk_ref[...]) * (1.0 / GLA_NORMALIZER)
    b = _dot(tri_incl, log_a, HI)
    lane = _iota((1, LANE), 1)
    bd = _idiv(_iota((2 * GLA_DV, LANE), 0), GLA_DV) == _idiv(_iota((2 * GLA_DV, LANE), 1), GLA_DK)
    sub = 16
    o_heads = [None] * GLA_HEADS
    nsub = c // sub
    vb = _bf(v_ref[...])
    scores = {}
    o_inter = []
    for p in range(GLA_HEADS // 2):
        sl = slice(p * LANE, (p + 1) * LANE)
        qp = q_ref[:, sl] * (GLA_DK ** -0.5)
        kp = k_ref[:, sl]
        bp = b[:, sl]
        b_last = bp[c - 1:c]
        st = st_ref[p]
        o_inter.append(_dot_nt(_bf(qp * jnp.exp(bp)), _bf(st)))
        upd = _dot_tn(vb[:, 2 * p * GLA_DV:(2 * p + 2) * GLA_DV], _bf(kp * jnp.exp(b_last - bp)))
        st_ref[p] = st * jnp.exp(b_last) + jnp.where(bd, upd, 0.0)
        for i in range(nsub):
            rows = slice(i * sub, (i + 1) * sub)
            nrow = (i + 1) * sub
            bref = bp[i * sub - 1:i * sub] if i > 0 else jnp.zeros((1, LANE), F32)
            qi = qp[rows] * jnp.exp(bp[rows] - bref)
            ki = _bf(kp[:nrow] * jnp.exp(jnp.minimum(bref - bp[:nrow], 80.0)))
            causal = _iota((sub, nrow), 1) <= (_iota((sub, nrow), 0) + i * sub)
            for hh in range(2):
                head_lanes = _idiv(lane, GLA_DK) == hh
                s = _dot_nt(_bf(jnp.where(head_lanes, qi, 0.0)), ki)
                scores[2 * p + hh, i] = _bf(jnp.where(causal, s, 0.0))
    for h in range(GLA_HEADS):
        intra = [_dot(scores[h, i], vb[:(i + 1) * sub, h * GLA_DV:(h + 1) * GLA_DV]) for i in range(nsub)]
        hh = h % 2
        o_heads[h] = o_inter[h // 2][:, hh * GLA_DV:(hh + 1) * GLA_DV] + jnp.concatenate(intra, axis=0)
    gain = gain_ref[...]
    for h in range(GLA_HEADS):
        o = o_heads[h]
        y = o * lax.rsqrt(jnp.mean(o * o, axis=-1, keepdims=True) + 1e-6) * gain
        g = g_ref[:, h * GLA_DV:(h + 1) * GLA_DV]
        o_ref[:, h * GLA_DV:(h + 1) * GLA_DV] = _bf(y * (g * _sigmoid(g)))


def gla_mixer(proj, w_gk, b_gk, out_gain, layer, batch, seq):
    nt = seq // CHUNK
    def col(off, width):
        return pl.BlockSpec((CHUNK, width), lambda b, t: (b * nt + t, off // width))
    return pl.pallas_call(
        _gla_kernel,
        grid=(batch, nt),
        in_specs=[col(P_GLA_Q, 256), col(P_GLA_K, 256), col(P_GLA_V, 512), col(P_GLA_G, 512), col(P_GLA_LR, 128),
                  pl.BlockSpec((None, LANE, 256), lambda b, t: (layer, 0, 0)),
                  pl.BlockSpec((None, 1, 256), lambda b, t: (layer, 0, 0)),
                  pl.BlockSpec((None, 1, GLA_DV), lambda b, t: (layer, 0, 0))],
        out_specs=pl.BlockSpec((CHUNK, 512), lambda b, t: (b * nt + t, 0)),
        out_shape=jax.ShapeDtypeStruct((batch * seq, 512), BF16),
        scratch_shapes=[pltpu.VMEM((GLA_HEADS // 2, 2 * GLA_DV, LANE), F32)],
        compiler_params=_cparams("parallel", "arbitrary"),
        name="gla_mixer",
    )(proj, proj, proj, proj, proj, w_gk, b_gk, out_gain)


def _lru_kernel(x_ref, gb_ref, cw_ref, cb_ref, wa_ref, ba_ref, wi_ref, bi_ref, lam_ref, o_ref, xc_ref, h_ref):
    t = x_ref.shape[0]

    @pl.when(pl.program_id(1) == 0)
    def _():
        xc_ref[...] = jnp.zeros_like(xc_ref)
        h_ref[...] = jnp.zeros_like(h_ref)

    x = x_ref[...]
    top = xc_ref[...]
    xc_ref[...] = x[t - 8:]
    cw = cw_ref[...]
    xc = (cw[0:1] * _shift_rows(x, 3, top) + cw[1:2] * _shift_rows(x, 2, top) + cw[2:3] * _shift_rows(x, 1, top)
          + cw[3:4] * x + cb_ref[...])
    xcb = _bf(xc)
    ra, ri = [], []
    for n in range(LRU_BLOCKS):
        blk = xcb[:, n * LANE:(n + 1) * LANE]
        ra.append(_dot(blk, wa_ref[n]))
        ri.append(_dot(blk, wi_ref[n]))
    r = _sigmoid(jnp.concatenate(ra, axis=1) + ba_ref[...])
    gi = _sigmoid(jnp.concatenate(ri, axis=1) + bi_ref[...])
    log_a = LRU_C * r * _log_sigmoid(lam_ref[...])
    a = jnp.exp(log_a)
    u = jnp.sqrt(-jnp.tanh(log_a) * (a * a + 1.0)) * (gi * xc)
    row = _iota((t, LRU_WIDTH), 0)
    k = 1
    while k < t:
        a_sh = jnp.where(row < k, 1.0, pltpu.roll(a, k, 0))
        u_sh = jnp.where(row < k, 0.0, pltpu.roll(u, k, 0))
        u = u + a * u_sh
        a = a * a_sh
        k *= 2
    h = a * h_ref[0:1] + u
    h_ref[...] = jnp.broadcast_to(h[t - 1:t], h_ref.shape)
    o_ref[...] = _bf(h * _gelu_tanh(gb_ref[...]))


def lru_mixer(proj, conv_w, conv_b, w_a, b_a, w_i, b_i, lam, layer, batch, seq, tt):
    nt = seq // tt
    vec = pl.BlockSpec((None, 1, LRU_WIDTH), lambda b, t: (layer, 0, 0))
    wblk = pl.BlockSpec((None, LRU_BLOCKS, LANE, LANE), lambda b, t: (layer, 0, 0, 0))
    return pl.pallas_call(
        _lru_kernel,
        grid=(batch, nt),
        in_specs=[pl.BlockSpec((tt, 512), lambda b, t: (b * nt + t, P_LRU_X // 512)),
                  pl.BlockSpec((tt, 512), lambda b, t: (b * nt + t, P_LRU_G // 512)),
                  pl.BlockSpec((None, 8, LRU_WIDTH), lambda b, t: (layer, 0, 0)), vec, wblk, vec, wblk, vec, vec],
        out_specs=pl.BlockSpec((tt, 512), lambda b, t: (b * nt + t, 0)),
        out_shape=jax.ShapeDtypeStruct((batch * seq, 512), BF16),
        scratch_shapes=[pltpu.VMEM((8, LRU_WIDTH), F32), pltpu.VMEM((8, LRU_WIDTH), F32)],
        compiler_params=_cparams("parallel", "arbitrary"),
        name="lru_mixer",
    )(proj, proj, conv_w, conv_b, w_a, b_a, w_i, b_i, lam)


def _rwkv_kernel(f_ref, mu_ref, w0_ref, wl_ref, a0_ref, al_ref, gl_ref, kk_ref, ka_ref, rk_ref, lnw_ref, lnb_ref,
                 o_ref, st_ref, prev_ref):
    c = CHUNK
    n = RWKV_DH

    @pl.when(pl.program_id(1) == 0)
    def _():
        st_ref[...] = jnp.zeros_like(st_ref)
        prev_ref[...] = jnp.zeros_like(prev_ref)

    feat = f_ref[...]
    prev = _shift_rows(feat, 1, prev_ref[...])
    prev_ref[...] = feat[c - 8:]
    xm = feat + (prev - feat) * mu_ref[...]
    r = xm[:, RW_R:RW_R + 512]
    k = xm[:, RW_K:RW_K + 512]
    v = xm[:, RW_V:RW_V + 512]
    log_w = -math.exp(-0.5) * _sigmoid(w0_ref[...] + _dot(_bf(jnp.tanh(xm[:, RW_XW:RW_XW + LANE])), wl_ref[...]))
    a = _sigmoid(a0_ref[...] + _dot(_bf(xm[:, RW_XA:RW_XA + LANE]), al_ref[...]))
    g = _dot(_bf(_sigmoid(xm[:, RW_XG:RW_XG + 256])), gl_ref[...])

    assert c == n and 2 * n == LANE
    head0 = _iota((1, LANE), 1) < n
    same_head = _idiv(_iota((LANE, LANE), 0), n) == _idiv(_iota((LANE, LANE), 1), n)
    head_ones = _bf(same_head.astype(F32))
    row = _iota((c, LANE), 0)
    col = jnp.bitwise_and(_iota((c, LANE), 1), n - 1)
    tri_incl = col <= row
    tri_strict = col < row
    tri_sq = _bf((_iota((c, c), 1) <= _iota((c, c), 0)).astype(F32))

    def stack(t):
        return jnp.concatenate([jnp.where(head0, t, 0.0), jnp.where(head0, 0.0, t)], axis=0)

    def seg_sum(t):
        hi = _bf(t)
        return _dot(hi, head_ones) + _dot(_bf(t - hi.astype(F32)), head_ones)

    lw_hi = _bf(log_w)
    lw_r = log_w - lw_hi.astype(F32)
    lw_mid = _bf(lw_r)
    cum_all = _dot(tri_sq, lw_hi) + _dot(tri_sq, lw_mid) + _dot(tri_sq, _bf(lw_r - lw_mid.astype(F32)))

    pairs = range(RWKV_HEADS // 2)
    sls = [slice(p * LANE, (p + 1) * LANE) for p in pairs]
    kk = [k[:, sl] * kk_ref[:, sl] for sl in sls]
    ss = [seg_sum(t * t) for t in kk]
    kk = [t / jnp.maximum(jnp.sqrt(s), 1e-12) for t, s in zip(kk, ss)]
    k2 = [k[:, sl] * (1.0 + (a[:, sl] - 1.0) * ka_ref[:, sl]) for sl in sls]
    e_last = [jnp.exp(cum_all[c - 1:c, sl]) for sl in sls]
    e_neg = [jnp.exp(-cum_all[:, sl]) for sl in sls]
    rt = [_bf(r[:, sl] * jnp.exp(cum_all[:, sl])) for sl in sls]
    kt = [_bf(kk[p] * jnp.exp(cum_all[:, sls[p]] - log_w[:, sls[p]])) for p in pairs]
    kb = [k2[p] * e_neg[p] for p in pairs]
    bb = [kk[p] * a[:, sls[p]] * e_neg[p] for p in pairs]
    kb2 = [_bf(stack(t)) for t in kb]
    bb2 = [_bf(stack(t)) for t in bb]
    vstk = [_bf(stack(v[:, sl])) for sl in sls]
    st = [st_ref[p] for p in pairs]
    stb = [_bf(t) for t in st]
    a_mat = [jnp.where(tri_strict, _dot_nt(kt[p], bb2[p]), 0.0) for p in pairs]
    b_mat = [jnp.where(tri_strict, _dot_nt(kt[p], kb2[p]), 0.0) for p in pairs]
    rr_mat = [jnp.where(tri_incl, _dot_nt(rt[p], kb2[p]), 0.0) for p in pairs]
    rb_mat = [jnp.where(tri_incl, _dot_nt(rt[p], bb2[p]), 0.0) for p in pairs]
    rhs = [_dot_nt(kt[p], stb[p]) + _dot(_bf(b_mat[p]), vstk[p]) for p in pairs]
    y0 = [_dot_nt(rt[p], stb[p]) + _dot(_bf(rr_mat[p]), vstk[p]) for p in pairs]
    nmat = [-t for t in a_mat]
    tm1 = list(nmat)
    pw = 1
    while 2 * pw < c:
        nmat = [_dot(_bf(t), _bf(stack(t))) for t in nmat]
        tm1 = [tm1[p] + nmat[p] + _dot(_bf(tm1[p]), _bf(stack(nmat[p]))) for p in pairs]
        pw *= 2
    u = [rhs[p] + _dot(_bf(tm1[p]), _bf(stack(rhs[p]))) for p in pairs]
    y = [y0[p] - _dot(_bf(rb_mat[p]), _bf(stack(u[p]))) for p in pairs]
    upd = [_dot_tn(_bf(jnp.concatenate([v[:, sls[p]], -u[p]], axis=0)),
                   _bf(jnp.concatenate([kb[p], bb[p]], axis=0) * e_last[p])) for p in pairs]
    for p in pairs:
        st_ref[p] = st[p] * e_last[p] + jnp.where(same_head, upd[p], 0.0)
    yc = [y[p] - seg_sum(y[p]) * (1.0 / n) for p in pairs]
    var = [seg_sum(t * t) * (1.0 / n) for t in yc]
    bonus = [seg_sum(r[:, sls[p]] * k2[p] * rk_ref[:, sls[p]]) for p in pairs]
    out = [(yc[p] * lax.rsqrt(var[p] + 64e-5) * lnw_ref[:, sls[p]] + lnb_ref[:, sls[p]] + bonus[p] * v[:, sls[p]])
           for p in pairs]
    o_ref[...] = _bf(jnp.concatenate(out, axis=1) * g)


def rwkv_mixer(proj, mu, w0, w_lora, a0, a_lora, g_lora, k_k, k_a, r_k, ln_w, ln_b, layer, batch, seq):
    nt = seq // CHUNK
    vec = pl.BlockSpec((None, 1, RWKV_WIDTH), lambda b, t: (layer, 0, 0))
    return pl.pallas_call(
        _rwkv_kernel,
        grid=(batch, nt),
        in_specs=[pl.BlockSpec((CHUNK, RW_TOTAL), lambda b, t: (b * nt + t, P_RWKV // RW_TOTAL)),
                  pl.BlockSpec((None, 1, RW_TOTAL), lambda b, t: (layer, 0, 0)),
                  vec, pl.BlockSpec((None, LANE, RWKV_WIDTH), lambda b, t: (layer, 0, 0)),
                  vec, pl.BlockSpec((None, LANE, RWKV_WIDTH), lambda b, t: (layer, 0, 0)),
                  pl.BlockSpec((None, RWKV_G_LORA, RWKV_WIDTH), lambda b, t: (layer, 0, 0)),
                  vec, vec, vec, vec, vec],
        out_specs=pl.BlockSpec((CHUNK, 512), lambda b, t: (b * nt + t, 0)),
        out_shape=jax.ShapeDtypeStruct((batch * seq, 512), BF16),
        scratch_shapes=[pltpu.VMEM((RWKV_HEADS // 2, LANE, LANE), F32), pltpu.VMEM((8, RW_TOTAL), F32)],
        compiler_params=_cparams("parallel", "arbitrary"),
        name="rwkv_mixer",
    )(proj, mu, w0, w_lora, a0, a_lora, g_lora, k_k, k_a, r_k, ln_w, ln_b)


def _head_rms(x, gain):
    return x * lax.rsqrt(jnp.mean(x * x, axis=-1, keepdims=True) + 1e-6) * gain


def _nsa_prep_kernel(q_ref, kv_ref, qg_ref, kg_ref, qn_ref, ks_ref, vs_ref, kw_ref, vw_ref):
    qg = qg_ref[...] * (NSA_DH ** -0.5)
    for h in range(NSA_HEADS):
        sl = slice(h * NSA_DH, (h + 1) * NSA_DH)
        qn_ref[:, sl] = _bf(_head_rms(q_ref[:, sl], qg))
    kg = kg_ref[...]
    ks_ref[...] = _bf(_head_rms(kv_ref[:, 2 * NSA_DH:3 * NSA_DH], kg))
    vs_ref[...] = _bf(kv_ref[:, 3 * NSA_DH:4 * NSA_DH])
    kw_ref[...] = _bf(_head_rms(kv_ref[:, 4 * NSA_DH:5 * NSA_DH], kg))
    vw_ref[...] = _bf(kv_ref[:, 5 * NSA_DH:6 * NSA_DH])


def nsa_prep(proj, q_gain, k_gain, layer, tt):
    m = proj.shape[0]
    gain = pl.BlockSpec((None, 1, NSA_DH), lambda i: (layer, 0, 0))
    kv_out = pl.BlockSpec((tt, NSA_DH), lambda i: (i, 0))
    kv_shape = jax.ShapeDtypeStruct((m, NSA_DH), BF16)
    return pl.pallas_call(
        _nsa_prep_kernel,
        grid=(m // tt,),
        in_specs=[pl.BlockSpec((tt, 512), lambda i: (i, P_NSA_Q // 512)),
                  pl.BlockSpec((tt, 768), lambda i: (i, P_NSA_KV // 768)), gain, gain],
        out_specs=[pl.BlockSpec((tt, 512), lambda i: (i, 0)), kv_out, kv_out, kv_out, kv_out],
        out_shape=[jax.ShapeDtypeStruct((m, 512), BF16), kv_shape, kv_shape, kv_shape, kv_shape],
        compiler_params=_cparams("parallel"),
        name="nsa_prep",
    )(proj, proj, q_gain, k_gain)


def _nsa_compress_kernel(kg_ref, vg_ref, pos_ref, k1_ref, k2_ref, v1_ref, v2_ref, gain_ref, kc_ref, vc_ref):
    nc = kg_ref.shape[0]
    half = NSA_CMP_STRIDE * NSA_DH
    pos = _bf(pos_ref[...])

    def compress(g_ref, w1_ref, w2_ref):
        grp = _bf(g_ref[...])
        first = _dot(grp, w1_ref[:half])
        second = _dot(grp, w1_ref[half:])
        const = _dot(pos, w1_ref[...])[0:1]
        hid = first + pltpu.roll(second, nc - 1, 0) + const
        return _dot(_bf(_gelu_tanh(hid)), w2_ref[...])

    kc_ref[...] = _bf(_head_rms(compress(kg_ref, k1_ref, k2_ref), gain_ref[...]))
    vc_ref[...] = _bf(compress(vg_ref, v1_ref, v2_ref))


def nsa_compress(kgrp, vgrp, pos, k1, k2, v1, v2, k_gain, layer):
    batch, nc, width = kgrp.shape
    grp = pl.BlockSpec((None, nc, width), lambda b: (b, 0, 0))
    w1 = pl.BlockSpec((None, 2 * width, NSA_DH), lambda b: (layer, 0, 0))
    w2 = pl.BlockSpec((None, NSA_DH, NSA_DH), lambda b: (layer, 0, 0))
    out = pl.BlockSpec((None, nc, NSA_DH), lambda b: (b, 0, 0))
    shape = jax.ShapeDtypeStruct((batch, nc, NSA_DH), BF16)
    return pl.pallas_call(
        _nsa_compress_kernel,
        grid=(batch,),
        in_specs=[grp, grp, pl.BlockSpec((None, 8, 2 * width), lambda b: (layer, 0, 0)), w1, w2, w1, w2,
                  pl.BlockSpec((None, 1, NSA_DH), lambda b: (layer, 0, 0))],
        out_specs=[out, out],
        out_shape=[shape, shape],
        compiler_params=_cparams("parallel"),
        name="nsa_compress",
    )(kgrp, vgrp, pos, k1, k2, v1, v2, k_gain)


def _nsa_cmp_kernel(cfar_ref, q_ref, kc_ref, vc_ref, band_ref, ovl_ref, gate_ref, o_ref, sel_ref):
    qb = NSA_QBLOCK
    nc = kc_ref.shape[0]
    i = pl.program_id(1)
    kc = kc_ref[...]
    vc = vc_ref[...]
    r = _iota((qb, nc), 0)
    ncol = _iota((qb, nc), 1)
    dist = qb * i + r - NSA_CMP_STRIDE * ncol - (NSA_CMP_LEN - 1)
    visible = dist >= 0
    lo = (qb // NSA_CMP_STRIDE) * i - 9
    in_band = (ncol >= lo) & (ncol <= lo + 15)
    gate = gate_ref[...]
    p_sum = jnp.zeros((qb, nc), F32)
    for h in range(NSA_HEADS):
        sl = slice(h * NSA_DH, (h + 1) * NSA_DH)
        band = jnp.concatenate([band_ref[h]] * (nc // LANE), axis=1)
        logit = _dot_nt(q_ref[:, sl], kc) + jnp.where(in_band, band, cfar_ref[h])
        logit = jnp.where(visible, logit, MASKED)
        mx = jnp.max(logit, axis=-1, keepdims=True)
        p = jnp.where(visible, jnp.exp(logit - mx), 0.0)
        p = p / jnp.maximum(jnp.sum(p, axis=-1, keepdims=True), 1e-30)
        p_sum = p_sum + p
        o_ref[:, sl] = _sigmoid(gate[:, h:h + 1]) * _dot(_bf(p), vc)
    p_hi = _bf(p_sum)
    p_lo = _bf(p_sum - p_hi.astype(F32))
    ovl = ovl_ref[...]
    score = _dot(p_hi, ovl) + _dot(p_lo, ovl)
    blk = _iota((qb, LANE), 1)
    pos = qb * i + _iota((qb, LANE), 0)
    cur = _idiv(pos, NSA_SEL_BLOCK)
    forced = (blk == 0) | (blk == cur) | (blk == cur - 1)
    work = jnp.where(forced, NEG_BIG, jnp.where(blk * NSA_SEL_BLOCK <= pos, score, -NEG_BIG))
    sel = jnp.zeros((qb, LANE), F32)
    blk_f = blk.astype(F32)
    for _ in range(NSA_SEL_TOPK):
        mx = jnp.max(work, axis=-1, keepdims=True)
        first = jnp.min(jnp.where(work == mx, blk_f, float(LANE)), axis=-1, keepdims=True)
        pick = blk_f == first
        sel = jnp.where(pick, 1.0, sel)
        work = jnp.where(pick, -jnp.inf, work)
    sel_ref[...] = _bf(sel)


def nsa_cmp_attention(cfar, qn, kc, vc, band, ovl, proj, batch, seq):
    nq = seq // NSA_QBLOCK
    nc = kc.shape[1]
    full = pl.BlockSpec((None, nc, NSA_DH), lambda b, i: (b, 0, 0))
    return pl.pallas_call(
        _nsa_cmp_kernel,
        grid=(batch, nq),
        in_specs=[pl.BlockSpec(memory_space=pltpu.SMEM),
                  pl.BlockSpec((NSA_QBLOCK, 512), lambda b, i: (b * nq + i, 0)), full, full,
                  pl.BlockSpec((None, NSA_HEADS, NSA_QBLOCK, LANE), lambda b, i: (i % 16, 0, 0, 0)),
                  pl.BlockSpec((nc, LANE), lambda b, i: (0, 0)),
                  pl.BlockSpec((NSA_QBLOCK, LANE), lambda b, i: (b * nq + i, P_NSA_GATE // LANE))],
        out_specs=[pl.BlockSpec((NSA_QBLOCK, 512), lambda b, i: (b * nq + i, 0)),
                   pl.BlockSpec((NSA_QBLOCK, LANE), lambda b, i: (b * nq + i, 0))],
        out_shape=[jax.ShapeDtypeStruct((batch * seq, 512), F32), jax.ShapeDtypeStruct((batch * seq, LANE), BF16)],
        compiler_params=_cparams("parallel", "arbitrary"),
        name="nsa_cmp_attention",
    )(cfar, qn, kc, vc, band, ovl, proj)


def _nsa_sel_win_kernel(q_ref, ks_ref, vs_ref, kw_ref, vw_ref, sel_ref, toep_ref, gate_ref, ocmp_ref,
                        o_ref, selt_ref, m_ref, l_ref, acc_ref):
    qb = NSA_QBLOCK
    nh = NSA_HEADS
    i = pl.program_id(1)
    qt = jnp.concatenate([_bf(q_ref[:, h * NSA_DH:(h + 1) * NSA_DH].astype(F32).T) for h in range(nh)], axis=1)
    selt = sel_ref[...].astype(F32).T
    selt_ref[...] = jnp.concatenate([jnp.where(selt > 0.5, 0.0, MASKED)] * nh, axis=1)
    add_diag = toep_ref[0]
    add_prev = toep_ref[1]
    add_edge = toep_ref[2]

    def reset():
        m_ref[...] = jnp.full_like(m_ref, M_INIT)
        l_ref[...] = jnp.zeros_like(l_ref)
        acc_ref[...] = jnp.zeros_like(acc_ref)

    def attend(k_tile, v_tile, add):
        s = _dot(k_tile, qt)
        if add is not None:
            s = s + add
        m_old = m_ref[...]
        m_new = jnp.maximum(m_old, jnp.max(s, axis=0, keepdims=True))
        p = jnp.exp(s - m_new)
        alpha = jnp.exp(m_old - m_new)
        l_ref[...] = alpha * l_ref[...] + jnp.sum(p, axis=0, keepdims=True)
        acc_ref[...] = alpha * acc_ref[...] + _dot_tn(v_tile, _bf(p))
        m_ref[...] = m_new

    def result():
        return acc_ref[...] / jnp.maximum(l_ref[...], 1e-30)

    def chosen(kt, ntile):
        half = NSA_SEL_BLOCK
        first = pl.multiple_of(kt * 2, 2)
        rows = [jnp.broadcast_to(selt_ref[pl.ds(first + j, 1), :], (half, nh * qb)) for j in range(2 * ntile)]
        return jnp.concatenate(rows, axis=0)

    def tile(ref, kt, ntile=1):
        return ref[pl.ds(pl.multiple_of(kt * qb, qb), ntile * qb), :]

    reset()
    nfar = jnp.maximum(i - 1, 0)

    def far_body(j, carry):
        kt = 2 * j
        attend(tile(ks_ref, kt, 2), tile(vs_ref, kt, 2), chosen(kt, 2))
        return carry

    lax.fori_loop(0, nfar // 2, far_body, 0)

    @pl.when(nfar % 2 == 1)
    def _():
        attend(tile(ks_ref, nfar - 1), tile(vs_ref, nfar - 1), chosen(nfar - 1, 1))

    kp = jnp.maximum(i - 1, 0)
    attend(tile(ks_ref, kp), tile(vs_ref, kp), chosen(kp, 1) + add_prev + jnp.where(i >= 1, 0.0, MASKED))
    attend(tile(ks_ref, i), tile(vs_ref, i), chosen(i, 1) + add_diag)
    o_sel = result()

    reset()
    nw = NSA_WINDOW // qb
    for d in range(nw + 1):
        jt = i - nw + d
        kt = jnp.maximum(jt, 0)
        add = jnp.where(jt >= 0, 0.0, MASKED)
        if d == nw:
            add = add + add_diag
        elif d == nw - 1:
            add = add + add_prev
        elif d == 0:
            add = add + add_edge
        attend(tile(kw_ref, kt), tile(vw_ref, kt), add)
    o_win = result()

    gate = gate_ref[...]
    for h in range(nh):
        sl = slice(h * NSA_DH, (h + 1) * NSA_DH)
        g_sel = _sigmoid(gate[:, nh + h:nh + h + 1])
        g_win = _sigmoid(gate[:, 2 * nh + h:2 * nh + h + 1])
        o_ref[:, sl] = _bf(ocmp_ref[:, sl] + g_sel * o_sel[:, sl].T + g_win * o_win[:, sl].T)


def nsa_sel_win_attention(qn, ks, vs, kw, vw, sel, toep_t, proj, ocmp, batch, seq):
    nq = seq // NSA_QBLOCK
    wide = NSA_HEADS * NSA_QBLOCK
    full = pl.BlockSpec((None, seq, NSA_DH), lambda b, i: (b, 0, 0))
    rows = lambda width, cb=0: pl.BlockSpec((NSA_QBLOCK, width), lambda b, i: (b * nq + i, cb))
    return pl.pallas_call(
        _nsa_sel_win_kernel,
        grid=(batch, nq),
        in_specs=[rows(512), full, full, full, full, rows(LANE),
                  pl.BlockSpec((3, NSA_QBLOCK, wide), lambda b, i: (0, 0, 0)),
                  rows(LANE, P_NSA_GATE // LANE), rows(512)],
        out_specs=rows(512),
        out_shape=jax.ShapeDtypeStruct((batch * seq, 512), BF16),
        scratch_shapes=[pltpu.VMEM((LANE, wide), F32), pltpu.VMEM((1, wide), F32), pltpu.VMEM((1, wide), F32),
                        pltpu.VMEM((NSA_DH, wide), F32)],
        compiler_params=_cparams("parallel", "arbitrary"),
        name="nsa_sel_win_attention",
    )(qn, ks, vs, kw, vw, sel, toep_t, proj, ocmp)


def _t5_bucket_np(dist):
    n = np.maximum(dist, 0)
    max_exact = NUM_BUCKETS // 2
    nf = np.maximum(n, 1).astype(np.float64)
    large = max_exact + (np.log(nf / max_exact) / math.log(MAX_DISTANCE / max_exact)
                         * (NUM_BUCKETS - max_exact)).astype(np.int64)
    large = np.minimum(large, NUM_BUCKETS - 1)
    return np.where(n < max_exact, n, large).astype(np.int32)


def _bias_tables(rel_bias):
    qb = NSA_QBLOCK
    r = np.arange(qb)[:, None]
    l = np.arange(qb)[None, :]
    toep_idx = np.stack([_t5_bucket_np(r - l), _t5_bucket_np(qb + r - l)])
    toep = jnp.transpose(rel_bias[toep_idx], (3, 0, 1, 2))
    per = qb // NSA_CMP_STRIDE
    band_idx = np.zeros((16, qb, LANE), np.int32)
    for im in range(16):
        base = per * im - 9
        n = base + ((np.arange(LANE) - base) % LANE)
        dist = qb * im + r - NSA_CMP_STRIDE * n[None, :] - (NSA_CMP_LEN - 1)
        band_idx[im] = _t5_bucket_np(dist)
    band = jnp.transpose(rel_bias[band_idx], (0, 3, 1, 2))
    cfar = rel_bias[NUM_BUCKETS - 1]
    near = jnp.transpose(toep - cfar[:, None, None, None], (1, 3, 0, 2)).reshape(2, qb, NSA_HEADS * qb)
    key = np.arange(qb)[:, None]
    qry = np.tile(np.arange(qb), NSA_HEADS)[None, :]
    diag = jnp.where(key <= qry, near[0], MASKED)
    edge = jnp.asarray(np.where(qry < key, 0.0, MASKED), F32)
    toep_t = jnp.stack([diag, near[1], edge])
    return toep_t.astype(F32), band.astype(F32), cfar.astype(F32)


def _overlap_table(nc, seq):
    n_cmp = (seq - NSA_CMP_LEN) // NSA_CMP_STRIDE + 1
    n_sel = seq // NSA_SEL_BLOCK
    cs = np.arange(nc)[:, None] * NSA_CMP_STRIDE
    ss = np.arange(LANE)[None, :] * NSA_SEL_BLOCK
    ovl = (cs < ss + NSA_SEL_BLOCK) & (cs + NSA_CMP_LEN > ss)
    ovl &= (np.arange(nc)[:, None] < n_cmp) & (np.arange(LANE)[None, :] < n_sel)
    return jnp.asarray(ovl.astype(np.float32), dtype=BF16)


def _pad_axis(t, axis, size):
    pad = [(0, 0)] * t.ndim
    pad[axis] = (0, size - t.shape[axis])
    return jnp.pad(t, pad)


def _pack_w_in(w_in):
    nl, k, _ = w_in.shape
    src = {}
    start = 0
    names = ("gla_q", "gla_k", "gla_v", "gla_g", "gla_lr", "lru_x", "lru_g", "nsa_q", "nsa_kv", "nsa_gate", "rwkv",
             "gates")
    widths = (256, 256, 512, 512, GLA_LOWRANK, 512, 512, 512, 768, 12, 1984, N_BRANCH * D_MODEL)
    for name, wd in zip(names, widths):
        src[name] = (start, wd)
        start += wd
    def piece(name, width):
        s, wd = src[name]
        return _pad_axis(w_in[:, :, s:s + wd], 2, width)
    rs, _ = src["rwkv"]
    rw = jnp.concatenate([
        w_in[:, :, rs:rs + 1536],
        _pad_axis(w_in[:, :, rs + 1536:rs + 1632], 2, LANE),
        _pad_axis(w_in[:, :, rs + 1632:rs + 1728], 2, LANE),
        w_in[:, :, rs + 1728:rs + 1984]], axis=2)
    packed = jnp.concatenate([
        piece("gates", 8192), rw, piece("lru_x", 512), piece("lru_g", 512), piece("nsa_q", 512), piece("gla_v", 512),
        piece("gla_g", 512), piece("gla_q", 256), piece("nsa_kv", 768), piece("gla_k", 256), piece("gla_lr", LANE),
        piece("nsa_gate", LANE)], axis=2)
    assert packed.shape[2] == P_TOTAL
    return packed.astype(BF16)


def _pack_rwkv_vec(t):
    return jnp.concatenate([t[:, :1536], _pad_axis(t[:, 1536:1632], 1, LANE), _pad_axis(t[:, 1632:1728], 1, LANE),
                            t[:, 1728:1984]], axis=1)[:, None, :]


def _row(t):
    return t[:, None, :]


def kernel(x, rel_bias, attn_norm, ffn_norm, w_in, gla_w_gk, gla_b_gk, gla_out_norm, lru_conv_w, lru_conv_b, lru_w_a, lru_b_a, lru_w_i, lru_b_i, lru_lambda, nsa_cmp_pos, nsa_cmp_k1, nsa_cmp_k2, nsa_cmp_v1, nsa_cmp_v2, nsa_q_norm, nsa_k_norm, rwkv_mu, rwkv_w0, rwkv_w_lora, rwkv_a0, rwkv_a_lora, rwkv_g_lora, rwkv_k_k, rwkv_k_a, rwkv_r_k, rwkv_ln_w, rwkv_ln_b, w_branch, w_out, ffn_up, ffn_conv_w, ffn_conv_b, ffn_down):
    batch, seq, d = x.shape
    depth = w_in.shape[0]
    m = batch * seq
    nc = seq // NSA_CMP_STRIDE
    assert d == D_MODEL and seq % 2048 == 0

    w_in_p = _pack_w_in(w_in)
    w_branch_b = _bf(w_branch)
    w_out_b = _bf(w_out)
    ffn_up_b = _bf(ffn_up)
    ffn_down_b = _bf(ffn_down)
    gla_w_gk_p = _pad_axis(gla_w_gk, 1, LANE)
    lru_conv_w_p = _pad_axis(lru_conv_w, 1, 8)
    ffn_conv_w_p = _pad_axis(ffn_conv_w, 1, 8)
    rwkv_w_lora_p = _bf(_pad_axis(rwkv_w_lora, 1, LANE))
    rwkv_a_lora_p = _bf(_pad_axis(rwkv_a_lora, 1, LANE))
    cmp_pos_p = jnp.broadcast_to(nsa_cmp_pos.reshape(depth, 1, NSA_CMP_LEN * NSA_DH), (depth, 8, NSA_CMP_LEN * NSA_DH))
    toep_t, band, cfar = _bias_tables(rel_bias)
    ovl = _overlap_table(nc, seq)

    tm = min(1024, seq)
    xf = x.reshape(m, d)
    for l in range(depth):
        proj = norm_matmul(xf, _row(attn_norm), w_in_p, l, tm, 512)
        y_a = gla_mixer(proj, gla_w_gk_p, _row(gla_b_gk), _row(gla_out_norm), l, batch, seq)
        y_b = lru_mixer(proj, lru_conv_w_p, _row(lru_conv_b), _bf(lru_w_a), _row(lru_b_a), _bf(lru_w_i), _row(lru_b_i),
                        _row(lru_lambda), l, batch, seq, min(256, seq))
        qn, ks, vs, kw, vw = nsa_prep(proj, _row(nsa_q_norm), _row(nsa_k_norm), l, min(512, seq))
        kgrp = proj[:, P_NSA_KV:P_NSA_KV + NSA_DH].reshape(batch, nc, NSA_CMP_STRIDE * NSA_DH)
        vgrp = proj[:, P_NSA_KV + NSA_DH:P_NSA_KV + 2 * NSA_DH].reshape(batch, nc, NSA_CMP_STRIDE * NSA_DH)
        kc, vc = nsa_compress(kgrp, vgrp, cmp_pos_p, _bf(nsa_cmp_k1), _bf(nsa_cmp_k2), _bf(nsa_cmp_v1),
                              _bf(nsa_cmp_v2), _row(nsa_k_norm), l)
        ocmp, sel = nsa_cmp_attention(cfar, qn, kc, vc, band, ovl, proj, batch, seq)
        seq3 = lambda t: t.reshape(batch, seq, NSA_DH)
        y_c = nsa_sel_win_attention(qn, seq3(ks), seq3(vs), seq3(kw), seq3(vw), sel, toep_t, proj, ocmp, batch, seq)
        y_d = rwkv_mixer(proj, _pack_rwkv_vec(rwkv_mu), _row(rwkv_w0), rwkv_w_lora_p, _row(rwkv_a0), rwkv_a_lora_p,
                         _bf(rwkv_g_lora), _row(rwkv_k_k), _row(rwkv_k_a), _row(rwkv_r_k.reshape(depth, RWKV_WIDTH)),
                         _row(rwkv_ln_w), _row(rwkv_ln_b), l, batch, seq)
        merged = merge_branches((y_a, y_b, y_c, y_d), w_branch_b, proj, l, min(512, seq), 512)
        xf = matmul_residual(merged, w_out_b, xf, l, tm, 512)
        act = ffn_up_conv(xf, _row(ffn_norm), ffn_up_b, ffn_conv_w_p, _row(ffn_conv_b), l, seq, tm, 512)
        xf = matmul_residual(act, ffn_down_b, xf, l, tm, 512)
    return xf.reshape(batch, seq, d)
```

```python
import functools
import math

import numpy as np
import jax
import jax.numpy as jnp
from jax import lax
from jax.experimental import pallas as pl
from jax.experimental.pallas import tpu as pltpu

F32 = jnp.float32
BF16 = jnp.bfloat16
HI = lax.Precision.HIGHEST

LANE = 128
VMEM_LIMIT = 56 * 1024 * 1024

D_MODEL = 2048
N_BRANCH = 4
BRANCH_WIDTH = 512

GLA_HEADS, GLA_DK, GLA_DV, GLA_LOWRANK, GLA_NORMALIZER = 4, 64, 128, 16, 16.0
LRU_WIDTH, LRU_BLOCKS, LRU_CONV, LRU_C = 512, 4, 4, 8.0
NSA_HEADS, NSA_DH = 4, 128
NSA_CMP_LEN, NSA_CMP_STRIDE, NSA_SEL_BLOCK, NSA_SEL_TOPK, NSA_WINDOW, NSA_QBLOCK = 32, 16, 64, 16, 512, 128
RWKV_HEADS, RWKV_DH, RWKV_WIDTH = 8, 64, 512
RWKV_W_LORA, RWKV_A_LORA, RWKV_G_LORA = 96, 96, 256
NUM_BUCKETS, MAX_DISTANCE = 32, 128
D_FF, FFN_CONV = 5632, 3
NEG_BIG = 1e9
MASKED = -1e30
M_INIT = -1e20

P_GATES, P_RWKV, P_LRU_X, P_LRU_G, P_NSA_Q, P_GLA_V, P_GLA_G = 0, 8192, 10240, 10752, 11264, 11776, 12288
P_GLA_Q, P_NSA_KV, P_GLA_K, P_GLA_LR, P_NSA_GATE, P_TOTAL = 12800, 13056, 13824, 14080, 14208, 14336
RW_R, RW_K, RW_V, RW_XW, RW_XA, RW_XG, RW_TOTAL = 0, 512, 1024, 1536, 1664, 1792, 2048

CHUNK = 64


def _cparams(*sem):
    return pltpu.CompilerParams(dimension_semantics=sem, vmem_limit_bytes=VMEM_LIMIT)


def _dot(a, b, prec=None):
    return jnp.dot(a, b, preferred_element_type=F32, precision=prec)


def _dot_nt(a, b, prec=None):
    return lax.dot_general(a, b, (((1,), (1,)), ((), ())), preferred_element_type=F32, precision=prec)


def _dot_tn(a, b, prec=None):
    return lax.dot_general(a, b, (((0,), (0,)), ((), ())), preferred_element_type=F32, precision=prec)


def _bf(x):
    return x.astype(BF16)


def _sigmoid(x):
    return 1.0 / (1.0 + jnp.exp(-x))


def _log_sigmoid(x):
    return jnp.minimum(x, 0.0) - jnp.log(1.0 + jnp.exp(-jnp.abs(x)))


def _gelu_tanh(x):
    return 0.5 * x * (1.0 + jnp.tanh(math.sqrt(2.0 / math.pi) * (x + 0.044715 * (x * x * x))))


def _iota(shape, dim):
    return lax.broadcasted_iota(jnp.int32, shape, dim)


def _idiv(x, d):
    assert d & (d - 1) == 0
    return jnp.right_shift(x, d.bit_length() - 1)


def _shift_rows(x, s, top):
    xr = pltpu.roll(x, s, 0)
    tr = pltpu.roll(top, s, 0)
    row = _iota((8, x.shape[1]), 0)
    head = jnp.where(row < s, tr, xr[:8])
    return jnp.concatenate([head, xr[8:]], axis=0)


def _norm_mm_kernel(x_ref, g_ref, w_ref, o_ref, xn_ref):
    @pl.when(pl.program_id(1) == 0)
    def _():
        x = x_ref[...]
        ms = jnp.mean(x * x, axis=-1, keepdims=True)
        xn_ref[...] = _bf(x * lax.rsqrt(ms + 1e-6) * g_ref[...])

    o_ref[...] = _dot(xn_ref[...], w_ref[...]).astype(o_ref.dtype)


def norm_matmul(x, gain, w, layer, tm, tn):
    m, k = x.shape
    n = w.shape[2]
    return pl.pallas_call(
        _norm_mm_kernel,
        grid=(m // tm, n // tn),
        in_specs=[pl.BlockSpec((tm, k), lambda i, j: (i, 0)),
                  pl.BlockSpec((None, 1, k), lambda i, j: (layer, 0, 0)),
                  pl.BlockSpec((None, k, tn), lambda i, j: (layer, 0, j))],
        out_specs=pl.BlockSpec((tm, tn), lambda i, j: (i, j)),
        out_shape=jax.ShapeDtypeStruct((m, n), F32),
        scratch_shapes=[pltpu.VMEM((tm, k), BF16)],
        compiler_params=_cparams("parallel", "arbitrary"),
        name="norm_matmul",
    )(x, gain, w)


def _mm_res_kernel(a_ref, w_ref, r_ref, o_ref):
    o_ref[...] = r_ref[...] + _dot(a_ref[...], w_ref[...])


def matmul_residual(a, w, res, layer, tm, tn):
    m, k = a.shape
    n = w.shape[2]
    return pl.pallas_call(
        _mm_res_kernel,
        grid=(m // tm, n // tn),
        in_specs=[pl.BlockSpec((tm, k), lambda i, j: (i, 0)),
                  pl.BlockSpec((None, k, tn), lambda i, j: (layer, 0, j)),
                  pl.BlockSpec((tm, tn), lambda i, j: (i, j))],
        out_specs=pl.BlockSpec((tm, tn), lambda i, j: (i, j)),
        out_shape=jax.ShapeDtypeStruct((m, n), F32),
        compiler_params=_cparams("parallel", "arbitrary"),
        name="matmul_residual",
    )(a, w, res)


def _merge_kernel(ya_ref, yb_ref, yc_ref, yd_ref, wb_ref, g0_ref, g1_ref, g2_ref, g3_ref, o_ref):
    acc = None
    for n, (y_ref, g_ref) in enumerate(((ya_ref, g0_ref), (yb_ref, g1_ref), (yc_ref, g2_ref), (yd_ref, g3_ref))):
        t = _sigmoid(g_ref[...]) * _dot(y_ref[...], wb_ref[n])
        acc = t if acc is None else acc + t
    o_ref[...] = _bf(acc)


def merge_branches(ys, w_branch, proj, layer, tm, tn):
    m = ys[0].shape[0]
    nj = D_MODEL // tn
    y_spec = pl.BlockSpec((tm, BRANCH_WIDTH), lambda i, j: (i, 0))
    gate_specs = [pl.BlockSpec((tm, tn), functools.partial(lambda i, j, n: (i, (P_GATES + n * D_MODEL) // tn + j), n=n))
                  for n in range(N_BRANCH)]
    return pl.pallas_call(
        _merge_kernel,
        grid=(m // tm, nj),
        in_specs=[y_spec] * 4 + [pl.BlockSpec((None, N_BRANCH, BRANCH_WIDTH, tn), lambda i, j: (layer, 0, 0, j))]
        + gate_specs,
        out_specs=pl.BlockSpec((tm, tn), lambda i, j: (i, j)),
        out_shape=jax.ShapeDtypeStruct((m, D_MODEL), BF16),
        compiler_params=_cparams("parallel", "arbitrary"),
        name="merge_branches",
    )(*ys, w_branch, proj, proj, proj, proj)


def _ffn_up_kernel(x_ref, g_ref, wg_ref, wv_ref, cw_g_ref, cw_v_ref, cb_g_ref, cb_v_ref, o_ref,
                   xn_ref, carry_ref, *, tiles_per_seq):
    i, j = pl.program_id(0), pl.program_id(1)

    @pl.when(j == 0)
    def _():
        x = x_ref[...]
        ms = jnp.mean(x * x, axis=-1, keepdims=True)
        xn_ref[...] = _bf(x * lax.rsqrt(ms + 1e-6) * g_ref[...])

    first = (i % tiles_per_seq) == 0

    @pl.when(first)
    def _():
        carry_ref[j] = jnp.zeros(carry_ref.shape[1:], F32)

    xn = xn_ref[...]
    outs = []
    for half, (w_ref, cw_ref, cb_ref) in enumerate(((wg_ref, cw_g_ref, cb_g_ref), (wv_ref, cw_v_ref, cb_v_ref))):
        u = _dot(xn, w_ref[...])
        top = carry_ref[j, half]
        carry_ref[j, half] = u[u.shape[0] - 8:]
        cw = cw_ref[...]
        outs.append(cw[0:1] * _shift_rows(u, 2, top) + cw[1:2] * _shift_rows(u, 1, top) + cw[2:3] * u + cb_ref[...])
    gate, val = outs
    o_ref[...] = _bf(gate * _sigmoid(gate) * val)


def ffn_up_conv(x, gain, w_up, conv_w, conv_b, layer, seq, tm, tn):
    m, k = x.shape
    nj = D_FF // tn
    kern = functools.partial(_ffn_up_kernel, tiles_per_seq=seq // tm)
    return pl.pallas_call(
        kern,
        grid=(m // tm, nj),
        in_specs=[pl.BlockSpec((tm, k), lambda i, j: (i, 0)),
                  pl.BlockSpec((None, 1, k), lambda i, j: (layer, 0, 0)),
                  pl.BlockSpec((None, k, tn), lambda i, j: (layer, 0, j)),
                  pl.BlockSpec((None, k, tn), lambda i, j: (layer, 0, nj + j)),
                  pl.BlockSpec((None, 8, tn), lambda i, j: (layer, 0, j)),
                  pl.BlockSpec((None, 8, tn), lambda i, j: (layer, 0, nj + j)),
                  pl.BlockSpec((None, 1, tn), lambda i, j: (layer, 0, j)),
                  pl.BlockSpec((None, 1, tn), lambda i, j: (layer, 0, nj + j))],
        out_specs=pl.BlockSpec((tm, tn), lambda i, j: (i, j)),
        out_shape=jax.ShapeDtypeStruct((m, D_FF), BF16),
        scratch_shapes=[pltpu.VMEM((tm, k), BF16), pltpu.VMEM((nj, 2, 8, tn), F32)],
        compiler_params=_cparams("arbitrary", "arbitrary"),
        name="ffn_up",
    )(x, gain, w_up, w_up, conv_w, conv_w, conv_b, conv_b)


def _gla_kernel(q_ref, k_ref, v_ref, g_ref, lr_ref, wgk_ref, bgk_ref, gain_ref, o_ref, st_ref):
    c = CHUNK

    @pl.when(pl.program_id(1) == 0)
    def _():
        st_ref[...] = jnp.zeros_like(st_ref)

    row = _iota((c, c), 0)
    col = _iota((c, c), 1)
    tri_incl = (col <= row).astype(F32)
    log_a = _log_sigmoid(_dot(lr_ref[...], wgk_ref[...], HI) + bgk_ref[...]) * (1.0 / GLA_NORMALIZER)
    b = _dot(tri_incl, log_a, HI)
    lane = _iota((1, LANE), 1)
    bd = _idiv(_iota((2 * GLA_DV, LANE), 0), GLA_DV) == _idiv(_iota((2 * GLA_DV, LANE), 1), GLA_DK)
    sub = 16
    o_heads = [None] * GLA_HEADS
    nsub = c // sub
    vb = _bf(v_ref[...])
    scores = {}
    o_inter = []
    for p in range(GLA_HEADS // 2):
        sl = slice(p * LANE, (p + 1) * LANE)
        qp = q_ref[:, sl] * (GLA_DK ** -0.5)
        kp = k_ref[:, sl]
        bp = b[:, sl]
        b_last = bp[c - 1:c]
        st = st_ref[p]
        o_inter.append(_dot_nt(_bf(qp * jnp.exp(bp)), _bf(st)))
        upd = _dot_tn(vb[:, 2 * p * GLA_DV:(2 * p + 2) * GLA_DV], _bf(kp * jnp.exp(b_last - bp)))
        st_ref[p] = st * jnp.exp(b_last) + jnp.where(bd, upd, 0.0)
        for i in range(nsub):
            rows = slice(i * sub, (i + 1) * sub)
            nrow = (i + 1) * sub
            bref = bp[i * sub - 1:i * sub] if i > 0 else jnp.zeros((1, LANE), F32)
            qi = qp[rows] * jnp.exp(bp[rows] - bref)
            ki = _bf(kp[:nrow] * jnp.exp(jnp.minimum(bref - bp[:nrow], 80.0)))
            causal = _iota((sub, nrow), 1) <= (_iota((sub, nrow), 0) + i * sub)
            for hh in range(2):
                head_lanes = _idiv(lane, GLA_DK) == hh
                s = _dot_nt(_bf(jnp.where(head_lanes, qi, 0.0)), ki)
                scores[2 * p + hh, i] = _bf(jnp.where(causal, s, 0.0))
    for h in range(GLA_HEADS):
        intra = [_dot(scores[h, i], vb[:(i + 1) * sub, h * GLA_DV:(h + 1) * GLA_DV]) for i in range(nsub)]
        hh = h % 2
        o_heads[h] = o_inter[h // 2][:, hh * GLA_DV:(hh + 1) * GLA_DV] + jnp.concatenate(intra, axis=0)
    gain = gain_ref[...]
    for h in range(GLA_HEADS):
        o = o_heads[h]
        y = o * lax.rsqrt(jnp.mean(o * o, axis=-1, keepdims=True) + 1e-6) * gain
        g = g_ref[:, h * GLA_DV:(h + 1) * GLA_DV]
        o_ref[:, h * GLA_DV:(h + 1) * GLA_DV] = _bf(y * (g * _sigmoid(g)))


def gla_mixer(proj, w_gk, b_gk, out_gain, layer, batch, seq):
    nt = seq // CHUNK
    def col(off, width):
        return pl.BlockSpec((CHUNK, width), lambda b, t: (b * nt + t, off // width))
    return pl.pallas_call(
        _gla_kernel,
        grid=(batch, nt),
        in_specs=[col(P_GLA_Q, 256), col(P_GLA_K, 256), col(P_GLA_V, 512), col(P_GLA_G, 512), col(P_GLA_LR, 128),
                  pl.BlockSpec((None, LANE, 256), lambda b, t: (layer, 0, 0)),
                  pl.BlockSpec((None, 1, 256), lambda b, t: (layer, 0, 0)),
                  pl.BlockSpec((None, 1, GLA_DV), lambda b, t: (layer, 0, 0))],
        out_specs=pl.BlockSpec((CHUNK, 512), lambda b, t: (b * nt + t, 0)),
        out_shape=jax.ShapeDtypeStruct((batch * seq, 512), BF16),
        scratch_shapes=[pltpu.VMEM((GLA_HEADS // 2, 2 * GLA_DV, LANE), F32)],
        compiler_params=_cparams("parallel", "arbitrary"),
        name="gla_mixer",
    )(proj, proj, proj, proj, proj, w_gk, b_gk, out_gain)


def _lru_kernel(x_ref, gb_ref, cw_ref, cb_ref, wa_ref, ba_ref, wi_ref, bi_ref, lam_ref, o_ref, xc_ref, h_ref):
    t = x_ref.shape[0]

    @pl.when(pl.program_id(1) == 0)
    def _():
        xc_ref[...] = jnp.zeros_like(xc_ref)
        h_ref[...] = jnp.zeros_like(h_ref)

    x = x_ref[...]
    top = xc_ref[...]
    xc_ref[...] = x[t - 8:]
    cw = cw_ref[...]
    xc = (cw[0:1] * _shift_rows(x, 3, top) + cw[1:2] * _shift_rows(x, 2, top) + cw[2:3] * _shift_rows(x, 1, top)
          + cw[3:4] * x + cb_ref[...])
    xcb = _bf(xc)
    ra, ri = [], []
    for n in range(LRU_BLOCKS):
        blk = xcb[:, n * LANE:(n + 1) * LANE]
        ra.append(_dot(blk, wa_ref[n]))
        ri.append(_dot(blk, wi_ref[n]))
    r = _sigmoid(jnp.concatenate(ra, axis=1) + ba_ref[...])
    gi = _sigmoid(jnp.concatenate(ri, axis=1) + bi_ref[...])
    log_a = LRU_C * r * _log_sigmoid(lam_ref[...])
    a = jnp.exp(log_a)
    u = jnp.sqrt(-jnp.tanh(log_a) * (a * a + 1.0)) * (gi * xc)
    row = _iota((t, LRU_WIDTH), 0)
    k = 1
    while k < t:
        a_sh = jnp.where(row < k, 1.0, pltpu.roll(a, k, 0))
        u_sh = jnp.where(row < k, 0.0, pltpu.roll(u, k, 0))
        u = u + a * u_sh
        a = a * a_sh
        k *= 2
    h = a * h_ref[0:1] + u
    h_ref[...] = jnp.broadcast_to(h[t - 1:t], h_ref.shape)
    o_ref[...] = _bf(h * _gelu_tanh(gb_ref[...]))


def lru_mixer(proj, conv_w, conv_b, w_a, b_a, w_i, b_i, lam, layer, batch, seq, tt):
    nt = seq // tt
    vec = pl.BlockSpec((None, 1, LRU_WIDTH), lambda b, t: (layer, 0, 0))
    wblk = pl.BlockSpec((None, LRU_BLOCKS, LANE, LANE), lambda b, t: (layer, 0, 0, 0))
    return pl.pallas_call(
        _lru_kernel,
        grid=(batch, nt),
        in_specs=[pl.BlockSpec((tt, 512), lambda b, t: (b * nt + t, P_LRU_X // 512)),
                  pl.BlockSpec((tt, 512), lambda b, t: (b * nt + t, P_LRU_G // 512)),
                  pl.BlockSpec((None, 8, LRU_WIDTH), lambda b, t: (layer, 0, 0)), vec, wblk, vec, wblk, vec, vec],
        out_specs=pl.BlockSpec((tt, 512), lambda b, t: (b * nt + t, 0)),
        out_shape=jax.ShapeDtypeStruct((batch * seq, 512), BF16),
        scratch_shapes=[pltpu.VMEM((8, LRU_WIDTH), F32), pltpu.VMEM((8, LRU_WIDTH), F32)],
        compiler_params=_cparams("parallel", "arbitrary"),
        name="lru_mixer",
    )(proj, proj, conv_w, conv_b, w_a, b_a, w_i, b_i, lam)


def _rwkv_kernel(f_ref, mu_ref, w0_ref, wl_ref, a0_ref, al_ref, gl_ref, kk_ref, ka_ref, rk_ref, lnw_ref, lnb_ref,
                 o_ref, st_ref, prev_ref):
    c = CHUNK
    n = RWKV_DH

    nb = f_ref.shape[0]

    @pl.when(pl.program_id(0) == 0)
    def _():
        st_ref[...] = jnp.zeros_like(st_ref)
        prev_ref[...] = jnp.zeros_like(prev_ref)

    feats = [f_ref[b] for b in range(nb)]
    prev = jnp.concatenate([_shift_rows(feats[b], 1, prev_ref[b]) for b in range(nb)], axis=0)
    for b in range(nb):
        prev_ref[b] = feats[b][c - 8:]
    feat = jnp.concatenate(feats, axis=0)
    xm = feat + (prev - feat) * mu_ref[...]
    r = xm[:, RW_R:RW_R + 512]
    k = xm[:, RW_K:RW_K + 512]
    v = xm[:, RW_V:RW_V + 512]
    log_w = -math.exp(-0.5) * _sigmoid(w0_ref[...] + _dot(_bf(jnp.tanh(xm[:, RW_XW:RW_XW + LANE])), wl_ref[...]))
    a = _sigmoid(a0_ref[...] + _dot(_bf(xm[:, RW_XA:RW_XA + LANE]), al_ref[...]))
    g = _dot(_bf(_sigmoid(xm[:, RW_XG:RW_XG + 256])), gl_ref[...])

    assert c == n and 2 * n == LANE
    head0 = _iota((1, LANE), 1) < n
    same_head = _idiv(_iota((LANE, LANE), 0), n) == _idiv(_iota((LANE, LANE), 1), n)
    head_ones = _bf(same_head.astype(F32))
    row = _iota((c, LANE), 0)
    col = jnp.bitwise_and(_iota((c, LANE), 1), n - 1)
    tri_incl = col <= row
    tri_strict = col < row
    rr_ = _iota((nb * c, nb * c), 0)
    cc_ = _iota((nb * c, nb * c), 1)
    tri_sq = _bf(((cc_ <= rr_) & (_idiv(cc_, c) == _idiv(rr_, c))).astype(F32))

    def stack(t):
        return jnp.concatenate([jnp.where(head0, t, 0.0), jnp.where(head0, 0.0, t)], axis=0)

    def seg_sum(t):
        hi = _bf(t)
        return _dot(hi, head_ones) + _dot(_bf(t - hi.astype(F32)), head_ones)

    lw_hi = _bf(log_w)
    lw_r = log_w - lw_hi.astype(F32)
    lw_mid = _bf(lw_r)
    cum_all = _dot(tri_sq, lw_hi) + _dot(tri_sq, lw_mid) + _dot(tri_sq, _bf(lw_r - lw_mid.astype(F32)))

    pairs = range(RWKV_HEADS // 2)
    sls = [slice(p * LANE, (p + 1) * LANE) for p in pairs]
    kk = [k[:, sl] * kk_ref[:, sl] for sl in sls]
    ss = [seg_sum(t * t) for t in kk]
    kk = [t / jnp.maximum(jnp.sqrt(s), 1e-12) for t, s in zip(kk, ss)]
    k2 = [k[:, sl] * (1.0 + (a[:, sl] - 1.0) * ka_ref[:, sl]) for sl in sls]
    e_neg = [jnp.exp(-cum_all[:, sl]) for sl in sls]
    rt_all = [_bf(r[:, sl] * jnp.exp(cum_all[:, sl])) for sl in sls]
    kt_all = [_bf(kk[p] * jnp.exp(cum_all[:, sls[p]] - log_w[:, sls[p]])) for p in pairs]
    kb_all = [k2[p] * e_neg[p] for p in pairs]
    bb_all = [kk[p] * a[:, sls[p]] * e_neg[p] for p in pairs]
    chains = [(b, p) for b in range(nb) for p in pairs]
    nch = range(len(chains))
    rows = [slice(b * c, (b + 1) * c) for b, _ in chains]
    e_last = [jnp.exp(cum_all[b * c + c - 1:(b + 1) * c, sls[p]]) for b, p in chains]
    rt = [rt_all[p][rows[i]] for i, (_, p) in enumerate(chains)]
    kt = [kt_all[p][rows[i]] for i, (_, p) in enumerate(chains)]
    kb = [kb_all[p][rows[i]] for i, (_, p) in enumerate(chains)]
    bb = [bb_all[p][rows[i]] for i, (_, p) in enumerate(chains)]
    vv = [v[rows[i], sls[p]] for i, (_, p) in enumerate(chains)]
    kb2 = [_bf(stack(t)) for t in kb]
    bb2 = [_bf(stack(t)) for t in bb]
    vstk = [_bf(stack(t)) for t in vv]
    st = [st_ref[i] for i in nch]
    stb = [_bf(t) for t in st]
    a_mat = [jnp.where(tri_strict, _dot_nt(kt[i], bb2[i]), 0.0) for i in nch]
    b_mat = [jnp.where(tri_strict, _dot_nt(kt[i], kb2[i]), 0.0) for i in nch]
    rr_mat = [jnp.where(tri_incl, _dot_nt(rt[i], kb2[i]), 0.0) for i in nch]
    rb_mat = [jnp.where(tri_incl, _dot_nt(rt[i], bb2[i]), 0.0) for i in nch]
    rhs = [_dot_nt(kt[i], stb[i]) + _dot(_bf(b_mat[i]), vstk[i]) for i in nch]
    y0 = [_dot_nt(rt[i], stb[i]) + _dot(_bf(rr_mat[i]), vstk[i]) for i in nch]
    nmat = [-t for t in a_mat]
    tm1 = list(nmat)
    pw = 1
    while 2 * pw < c:
        nmat = [_dot(_bf(t), _bf(stack(t))) for t in nmat]
        tm1 = [tm1[i] + nmat[i] + _dot(_bf(tm1[i]), _bf(stack(nmat[i]))) for i in nch]
        pw *= 2
    u = [rhs[i] + _dot(_bf(tm1[i]), _bf(stack(rhs[i]))) for i in nch]
    y = [y0[i] - _dot(_bf(rb_mat[i]), _bf(stack(u[i]))) for i in nch]
    upd = [_dot_tn(_bf(jnp.concatenate([vv[i], -u[i]], axis=0)),
                   _bf(jnp.concatenate([kb[i], bb[i]], axis=0) * e_last[i])) for i in nch]
    for i in nch:
        st_ref[i] = st[i] * e_last[i] + jnp.where(same_head, upd[i], 0.0)
    y_all = [jnp.concatenate([y[b * len(pairs) + p] for b in range(nb)], axis=0) for p in pairs]
    yc = [y_all[p] - seg_sum(y_all[p]) * (1.0 / n) for p in pairs]
    var = [seg_sum(t * t) * (1.0 / n) for t in yc]
    bonus = [seg_sum(r[:, sls[p]] * k2[p] * rk_ref[:, sls[p]]) for p in pairs]
    out = [(yc[p] * lax.rsqrt(var[p] + 64e-5) * lnw_ref[:, sls[p]] + lnb_ref[:, sls[p]] + bonus[p] * v[:, sls[p]])
           for p in pairs]
    out = _bf(jnp.concatenate(out, axis=1) * g)
    for b in range(nb):
        o_ref[b] = out[b * c:(b + 1) * c]


def rwkv_mixer(proj, mu, w0, w_lora, a0, a_lora, g_lora, k_k, k_a, r_k, ln_w, ln_b, layer, batch, seq):
    nt = seq // CHUNK
    vec = pl.BlockSpec((None, 1, RWKV_WIDTH), lambda t: (layer, 0, 0))
    out = pl.pallas_call(
        _rwkv_kernel,
        grid=(nt,),
        in_specs=[pl.BlockSpec((batch, CHUNK, RW_TOTAL), lambda t: (0, t, P_RWKV // RW_TOTAL)),
                  pl.BlockSpec((None, 1, RW_TOTAL), lambda t: (layer, 0, 0)),
                  vec, pl.BlockSpec((None, LANE, RWKV_WIDTH), lambda t: (layer, 0, 0)),
                  vec, pl.BlockSpec((None, LANE, RWKV_WIDTH), lambda t: (layer, 0, 0)),
                  pl.BlockSpec((None, RWKV_G_LORA, RWKV_WIDTH), lambda t: (layer, 0, 0)),
                  vec, vec, vec, vec, vec],
        out_specs=pl.BlockSpec((batch, CHUNK, 512), lambda t: (0, t, 0)),
        out_shape=jax.ShapeDtypeStruct((batch, seq, 512), BF16),
        scratch_shapes=[pltpu.VMEM((batch * (RWKV_HEADS // 2), LANE, LANE), F32),
                        pltpu.VMEM((batch, 8, RW_TOTAL), F32)],
        compiler_params=_cparams("arbitrary"),
        name="rwkv_mixer",
    )(proj.reshape(batch, seq, proj.shape[1]), mu, w0, w_lora, a0, a_lora, g_lora, k_k, k_a, r_k, ln_w, ln_b)
    return out.reshape(batch * seq, 512)


def _head_rms(x, gain):
    return x * lax.rsqrt(jnp.mean(x * x, axis=-1, keepdims=True) + 1e-6) * gain


def _nsa_prep_kernel(q_ref, kv_ref, qg_ref, kg_ref, qn_ref, ks_ref, vs_ref, kw_ref, vw_ref):
    qg = qg_ref[...] * (NSA_DH ** -0.5)
    for h in range(NSA_HEADS):
        sl = slice(h * NSA_DH, (h + 1) * NSA_DH)
        qn_ref[:, sl] = _bf(_head_rms(q_ref[:, sl], qg))
    kg = kg_ref[...]
    ks_ref[...] = _bf(_head_rms(kv_ref[:, 2 * NSA_DH:3 * NSA_DH], kg))
    vs_ref[...] = _bf(kv_ref[:, 3 * NSA_DH:4 * NSA_DH])
    kw_ref[...] = _bf(_head_rms(kv_ref[:, 4 * NSA_DH:5 * NSA_DH], kg))
    vw_ref[...] = _bf(kv_ref[:, 5 * NSA_DH:6 * NSA_DH])


def nsa_prep(proj, q_gain, k_gain, layer, tt):
    m = proj.shape[0]
    gain = pl.BlockSpec((None, 1, NSA_DH), lambda i: (layer, 0, 0))
    kv_out = pl.BlockSpec((tt, NSA_DH), lambda i: (i, 0))
    kv_shape = jax.ShapeDtypeStruct((m, NSA_DH), BF16)
    return pl.pallas_call(
        _nsa_prep_kernel,
        grid=(m // tt,),
        in_specs=[pl.BlockSpec((tt, 512), lambda i: (i, P_NSA_Q // 512)),
                  pl.BlockSpec((tt, 768), lambda i: (i, P_NSA_KV // 768)), gain, gain],
        out_specs=[pl.BlockSpec((tt, 512), lambda i: (i, 0)), kv_out, kv_out, kv_out, kv_out],
        out_shape=[jax.ShapeDtypeStruct((m, 512), BF16), kv_shape, kv_shape, kv_shape, kv_shape],
        compiler_params=_cparams("parallel"),
        name="nsa_prep",
    )(proj, proj, q_gain, k_gain)


def _nsa_compress_kernel(kg_ref, vg_ref, pos_ref, k1_ref, k2_ref, v1_ref, v2_ref, gain_ref, kc_ref, vc_ref):
    nc = kg_ref.shape[0]
    half = NSA_CMP_STRIDE * NSA_DH
    pos = _bf(pos_ref[...])

    def compress(g_ref, w1_ref, w2_ref):
        grp = _bf(g_ref[...])
        first = _dot(grp, w1_ref[:half])
        second = _dot(grp, w1_ref[half:])
        const = _dot(pos, w1_ref[...])[0:1]
        hid = first + pltpu.roll(second, nc - 1, 0) + const
        return _dot(_bf(_gelu_tanh(hid)), w2_ref[...])

    kc_ref[...] = _bf(_head_rms(compress(kg_ref, k1_ref, k2_ref), gain_ref[...]))
    vc_ref[...] = _bf(compress(vg_ref, v1_ref, v2_ref))


def nsa_compress(kgrp, vgrp, pos, k1, k2, v1, v2, k_gain, layer):
    batch, nc, width = kgrp.shape
    grp = pl.BlockSpec((None, nc, width), lambda b: (b, 0, 0))
    w1 = pl.BlockSpec((None, 2 * width, NSA_DH), lambda b: (layer, 0, 0))
    w2 = pl.BlockSpec((None, NSA_DH, NSA_DH), lambda b: (layer, 0, 0))
    out = pl.BlockSpec((None, nc, NSA_DH), lambda b: (b, 0, 0))
    shape = jax.ShapeDtypeStruct((batch, nc, NSA_DH), BF16)
    return pl.pallas_call(
        _nsa_compress_kernel,
        grid=(batch,),
        in_specs=[grp, grp, pl.BlockSpec((None, 8, 2 * width), lambda b: (layer, 0, 0)), w1, w2, w1, w2,
                  pl.BlockSpec((None, 1, NSA_DH), lambda b: (layer, 0, 0))],
        out_specs=[out, out],
        out_shape=[shape, shape],
        compiler_params=_cparams("parallel"),
        name="nsa_compress",
    )(kgrp, vgrp, pos, k1, k2, v1, v2, k_gain)


CMP_GROUP = 4


def _nsa_cmp_kernel(cfar_ref, q_ref, kc_ref, vc_ref, *rest):
    band_refs = rest[:CMP_GROUP]
    ovl_ref, gate_ref, o_ref, sel_ref = rest[CMP_GROUP:]
    qb = NSA_QBLOCK
    nc = kc_ref.shape[0]
    kc = kc_ref[...]
    vc = vc_ref[...]
    ovl = ovl_ref[...]
    r = _iota((qb, nc), 0)
    ncol = _iota((qb, nc), 1)
    blk = _iota((qb, LANE), 1)
    works = []
    for g in range(CMP_GROUP):
        i = pl.program_id(1) * CMP_GROUP + g
        rows = slice(g * qb, (g + 1) * qb)
        dist = qb * i + r - NSA_CMP_STRIDE * ncol - (NSA_CMP_LEN - 1)
        visible = dist >= 0
        lo = (qb // NSA_CMP_STRIDE) * i - 9
        in_band = (ncol >= lo) & (ncol <= lo + 15)
        gate = gate_ref[rows, :]
        p_sum = jnp.zeros((qb, nc), F32)
        for h in range(NSA_HEADS):
            sl = slice(h * NSA_DH, (h + 1) * NSA_DH)
            band = jnp.concatenate([band_refs[g][h]] * (nc // LANE), axis=1)
            logit = _dot_nt(q_ref[rows, sl], kc) + jnp.where(in_band, band, cfar_ref[h])
            logit = jnp.where(visible, logit, MASKED)
            mx = jnp.max(logit, axis=-1, keepdims=True)
            p = jnp.where(visible, jnp.exp(logit - mx), 0.0)
            p = p / jnp.maximum(jnp.sum(p, axis=-1, keepdims=True), 1e-30)
            p_sum = p_sum + p
            o_ref[rows, sl] = _sigmoid(gate[:, h:h + 1]) * _dot(_bf(p), vc)
        p_hi = _bf(p_sum)
        score = _dot(p_hi, ovl) + _dot(_bf(p_sum - p_hi.astype(F32)), ovl)
        pos = qb * i + _iota((qb, LANE), 0)
        cur = _idiv(pos, NSA_SEL_BLOCK)
        forced = (blk == 0) | (blk == cur) | (blk == cur - 1)
        works.append(jnp.where(forced, NEG_BIG, jnp.where(blk * NSA_SEL_BLOCK <= pos, score, -NEG_BIG)))
    work = jnp.concatenate(works, axis=0)
    sel = jnp.zeros(work.shape, F32)
    blk_f = _iota(work.shape, 1).astype(F32)
    for _ in range(NSA_SEL_TOPK):
        mx = jnp.max(work, axis=-1, keepdims=True)
        first = jnp.min(jnp.where(work == mx, blk_f, float(LANE)), axis=-1, keepdims=True)
        pick = blk_f == first
        sel = jnp.where(pick, 1.0, sel)
        work = jnp.where(pick, -jnp.inf, work)
    sel_ref[...] = _bf(sel)


def nsa_cmp_attention(cfar, qn, kc, vc, band, ovl, proj, batch, seq):
    rows = CMP_GROUP * NSA_QBLOCK
    ns = seq // rows
    nc = kc.shape[1]
    full = pl.BlockSpec((None, nc, NSA_DH), lambda b, i: (b, 0, 0))
    band_specs = [pl.BlockSpec((None, NSA_HEADS, NSA_QBLOCK, LANE),
                               functools.partial(lambda b, i, g: ((i * CMP_GROUP + g) % 16, 0, 0, 0), g=g))
                  for g in range(CMP_GROUP)]
    return pl.pallas_call(
        _nsa_cmp_kernel,
        grid=(batch, ns),
        in_specs=[pl.BlockSpec(memory_space=pltpu.SMEM),
                  pl.BlockSpec((rows, 512), lambda b, i: (b * ns + i, 0)), full, full] + band_specs
        + [pl.BlockSpec((nc, LANE), lambda b, i: (0, 0)),
           pl.BlockSpec((rows, LANE), lambda b, i: (b * ns + i, P_NSA_GATE // LANE))],
        out_specs=[pl.BlockSpec((rows, 512), lambda b, i: (b * ns + i, 0)),
                   pl.BlockSpec((rows, LANE), lambda b, i: (b * ns + i, 0))],
        out_shape=[jax.ShapeDtypeStruct((batch * seq, 512), F32), jax.ShapeDtypeStruct((batch * seq, LANE), BF16)],
        compiler_params=_cparams("parallel", "arbitrary"),
        name="nsa_cmp_attention",
    )(cfar, qn, kc, vc, *([band] * CMP_GROUP), ovl, proj)


def _nsa_sel_win_kernel(q_ref, ks_ref, vs_ref, kw_ref, vw_ref, sel_ref, toep_ref, gate_ref, ocmp_ref,
                        o_ref, selt_ref, m_ref, l_ref, acc_ref):
    qb = NSA_QBLOCK
    nh = NSA_HEADS
    i = pl.program_id(1)
    qt = jnp.concatenate([_bf(q_ref[:, h * NSA_DH:(h + 1) * NSA_DH].astype(F32).T) for h in range(nh)], axis=1)
    selt = sel_ref[...].astype(F32).T
    selt_ref[...] = jnp.concatenate([jnp.where(selt > 0.5, 0.0, MASKED)] * nh, axis=1)
    add_diag = toep_ref[0]
    add_prev = toep_ref[1]
    add_edge = toep_ref[2]

    def reset():
        m_ref[...] = jnp.full_like(m_ref, M_INIT)
        l_ref[...] = jnp.zeros_like(l_ref)
        acc_ref[...] = jnp.zeros_like(acc_ref)

    def attend(k_tile, v_tile, add):
        s = _dot(k_tile, qt)
        if add is not None:
            s = s + add
        m_old = m_ref[...]
        m_new = jnp.maximum(m_old, jnp.max(s, axis=0, keepdims=True))
        p = jnp.exp(s - m_new)
        alpha = jnp.exp(m_old - m_new)
        l_ref[...] = alpha * l_ref[...] + jnp.sum(p, axis=0, keepdims=True)
        acc_ref[...] = alpha * acc_ref[...] + _dot_tn(v_tile, _bf(p))
        m_ref[...] = m_new

    def result():
        return acc_ref[...] / jnp.maximum(l_ref[...], 1e-30)

    def chosen(kt, ntile):
        half = NSA_SEL_BLOCK
        first = pl.multiple_of(kt * 2, 2)
        rows = [jnp.broadcast_to(selt_ref[pl.ds(first + j, 1), :], (half, nh * qb)) for j in range(2 * ntile)]
        return jnp.concatenate(rows, axis=0)

    def tile(ref, kt, ntile=1):
        return ref[pl.ds(pl.multiple_of(kt * qb, qb), ntile * qb), :]

    reset()
    nfar = jnp.maximum(i - 1, 0)

    npair = nfar // 2

    def far_scores(j):
        kt = 2 * j
        return _dot(tile(ks_ref, kt, 2), qt) + chosen(kt, 2)

    def far_values(j, p, alpha):
        acc_ref[...] = alpha * acc_ref[...] + _dot_tn(tile(vs_ref, 2 * j, 2), p)

    def far_body(j, carry):
        s, p_prev, alpha_prev = carry
        s_next = far_scores(jnp.minimum(j + 1, npair - 1))
        far_values(jnp.maximum(j - 1, 0), p_prev, alpha_prev)
        m_old = m_ref[...]
        m_new = jnp.maximum(m_old, jnp.max(s, axis=0, keepdims=True))
        p = jnp.exp(s - m_new)
        alpha = jnp.exp(m_old - m_new)
        l_ref[...] = alpha * l_ref[...] + jnp.sum(p, axis=0, keepdims=True)
        m_ref[...] = m_new
        return s_next, _bf(p), alpha

    start = (far_scores(0), jnp.zeros((2 * qb, nh * qb), BF16), jnp.ones((1, nh * qb), F32))
    _, p_last, alpha_last = lax.fori_loop(0, npair, far_body, start)
    far_values(jnp.maximum(npair - 1, 0), p_last, alpha_last)

    @pl.when(nfar % 2 == 1)
    def _():
        attend(tile(ks_ref, nfar - 1), tile(vs_ref, nfar - 1), chosen(nfar - 1, 1))

    kp = jnp.maximum(i - 1, 0)
    attend(tile(ks_ref, kp), tile(vs_ref, kp), chosen(kp, 1) + add_prev + jnp.where(i >= 1, 0.0, MASKED))
    attend(tile(ks_ref, i), tile(vs_ref, i), chosen(i, 1) + add_diag)
    o_sel = result()

    reset()
    nw = NSA_WINDOW // qb
    for d in range(nw + 1):
        jt = i - nw + d
        kt = jnp.maximum(jt, 0)
        add = jnp.where(jt >= 0, 0.0, MASKED)
        if d == nw:
            add = add + add_diag
        elif d == nw - 1:
            add = add + add_prev
        elif d == 0:
            add = add + add_edge
        attend(tile(kw_ref, kt), tile(vw_ref, kt), add)
    o_win = result()

    gate = gate_ref[...]
    for h in range(nh):
        sl = slice(h * NSA_DH, (h + 1) * NSA_DH)
        g_sel = _sigmoid(gate[:, nh + h:nh + h + 1])
        g_win = _sigmoid(gate[:, 2 * nh + h:2 * nh + h + 1])
        o_ref[:, sl] = _bf(ocmp_ref[:, sl] + g_sel * o_sel[:, sl].T + g_win * o_win[:, sl].T)


def nsa_sel_win_attention(qn, ks, vs, kw, vw, sel, toep_t, proj, ocmp, batch, seq):
    nq = seq // NSA_QBLOCK
    wide = NSA_HEADS * NSA_QBLOCK
    full = pl.BlockSpec((None, seq, NSA_DH), lambda b, i: (b, 0, 0))
    rows = lambda width, cb=0: pl.BlockSpec((NSA_QBLOCK, width), lambda b, i: (b * nq + i, cb))
    return pl.pallas_call(
        _nsa_sel_win_kernel,
        grid=(batch, nq),
        in_specs=[rows(512), full, full, full, full, rows(LANE),
                  pl.BlockSpec((3, NSA_QBLOCK, wide), lambda b, i: (0, 0, 0)),
                  rows(LANE, P_NSA_GATE // LANE), rows(512)],
        out_specs=rows(512),
        out_shape=jax.ShapeDtypeStruct((batch * seq, 512), BF16),
        scratch_shapes=[pltpu.VMEM((LANE, wide), F32), pltpu.VMEM((1, wide), F32), pltpu.VMEM((1, wide), F32),
                        pltpu.VMEM((NSA_DH, wide), F32)],
        compiler_params=_cparams("parallel", "arbitrary"),
        name="nsa_sel_win_attention",
    )(qn, ks, vs, kw, vw, sel, toep_t, proj, ocmp)


def _t5_bucket_np(dist):
    n = np.maximum(dist, 0)
    max_exact = NUM_BUCKETS // 2
    nf = np.maximum(n, 1).astype(np.float64)
    large = max_exact + (np.log(nf / max_exact) / math.log(MAX_DISTANCE / max_exact)
                         * (NUM_BUCKETS - max_exact)).astype(np.int64)
    large = np.minimum(large, NUM_BUCKETS - 1)
    return np.where(n < max_exact, n, large).astype(np.int32)


def _bias_tables(rel_bias):
    qb = NSA_QBLOCK
    r = np.arange(qb)[:, None]
    l = np.arange(qb)[None, :]
    toep_idx = np.stack([_t5_bucket_np(r - l), _t5_bucket_np(qb + r - l)])
    def lookup(idx):
        onehot = (jnp.asarray(idx.reshape(-1, 1)) == jnp.arange(NUM_BUCKETS)[None, :]).astype(F32)
        return jnp.dot(onehot, rel_bias.astype(F32), precision=HI).reshape(idx.shape + (NSA_HEADS,))

    toep = jnp.transpose(lookup(toep_idx), (3, 0, 1, 2))
    per = qb // NSA_CMP_STRIDE
    band_idx = np.zeros((16, qb, LANE), np.int32)
    for im in range(16):
        base = per * im - 9
        n = base + ((np.arange(LANE) - base) % LANE)
        dist = qb * im + r - NSA_CMP_STRIDE * n[None, :] - (NSA_CMP_LEN - 1)
        band_idx[im] = _t5_bucket_np(dist)
    band = jnp.transpose(lookup(band_idx), (0, 3, 1, 2))
    cfar = rel_bias[NUM_BUCKETS - 1]
    near = jnp.transpose(toep - cfar[:, None, None, None], (1, 3, 0, 2)).reshape(2, qb, NSA_HEADS * qb)
    key = np.arange(qb)[:, None]
    qry = np.tile(np.arange(qb), NSA_HEADS)[None, :]
    diag = jnp.where(key <= qry, near[0], MASKED)
    edge = jnp.asarray(np.where(qry < key, 0.0, MASKED), F32)
    toep_t = jnp.stack([diag, near[1], edge])
    return toep_t.astype(F32), band.astype(F32), cfar.astype(F32)


def _overlap_table(nc, seq):
    n_cmp = (seq - NSA_CMP_LEN) // NSA_CMP_STRIDE + 1
    n_sel = seq // NSA_SEL_BLOCK
    cs = np.arange(nc)[:, None] * NSA_CMP_STRIDE
    ss = np.arange(LANE)[None, :] * NSA_SEL_BLOCK
    ovl = (cs < ss + NSA_SEL_BLOCK) & (cs + NSA_CMP_LEN > ss)
    ovl &= (np.arange(nc)[:, None] < n_cmp) & (np.arange(LANE)[None, :] < n_sel)
    return jnp.asarray(ovl.astype(np.float32), dtype=BF16)


def _pad_axis(t, axis, size):
    pad = [(0, 0)] * t.ndim
    pad[axis] = (0, size - t.shape[axis])
    return jnp.pad(t, pad)


def _pack_w_in(w_in):
    nl, k, _ = w_in.shape
    src = {}
    start = 0
    names = ("gla_q", "gla_k", "gla_v", "gla_g", "gla_lr", "lru_x", "lru_g", "nsa_q", "nsa_kv", "nsa_gate", "rwkv",
             "gates")
    widths = (256, 256, 512, 512, GLA_LOWRANK, 512, 512, 512, 768, 12, 1984, N_BRANCH * D_MODEL)
    for name, wd in zip(names, widths):
        src[name] = (start, wd)
        start += wd
    def piece(name, width):
        s, wd = src[name]
        return _pad_axis(w_in[:, :, s:s + wd], 2, width)
    rs, _ = src["rwkv"]
    rw = jnp.concatenate([
        w_in[:, :, rs:rs + 1536],
        _pad_axis(w_in[:, :, rs + 1536:rs + 1632], 2, LANE),
        _pad_axis(w_in[:, :, rs + 1632:rs + 1728], 2, LANE),
        w_in[:, :, rs + 1728:rs + 1984]], axis=2)
    packed = jnp.concatenate([
        piece("gates", 8192), rw, piece("lru_x", 512), piece("lru_g", 512), piece("nsa_q", 512), piece("gla_v", 512),
        piece("gla_g", 512), piece("gla_q", 256), piece("nsa_kv", 768), piece("gla_k", 256), piece("gla_lr", LANE),
        piece("nsa_gate", LANE)], axis=2)
    assert packed.shape[2] == P_TOTAL
    return packed.astype(BF16)


def _pack_rwkv_vec(t):
    return jnp.concatenate([t[:, :1536], _pad_axis(t[:, 1536:1632], 1, LANE), _pad_axis(t[:, 1632:1728], 1, LANE),
                            t[:, 1728:1984]], axis=1)[:, None, :]


def _row(t):
    return t[:, None, :]


def kernel(x, rel_bias, attn_norm, ffn_norm, w_in, gla_w_gk, gla_b_gk, gla_out_norm, lru_conv_w, lru_conv_b, lru_w_a, lru_b_a, lru_w_i, lru_b_i, lru_lambda, nsa_cmp_pos, nsa_cmp_k1, nsa_cmp_k2, nsa_cmp_v1, nsa_cmp_v2, nsa_q_norm, nsa_k_norm, rwkv_mu, rwkv_w0, rwkv_w_lora, rwkv_a0, rwkv_a_lora, rwkv_g_lora, rwkv_k_k, rwkv_k_a, rwkv_r_k, rwkv_ln_w, rwkv_ln_b, w_branch, w_out, ffn_up, ffn_conv_w, ffn_conv_b, ffn_down):
    batch, seq, d = x.shape
    depth = w_in.shape[0]
    m = batch * seq
    nc = seq // NSA_CMP_STRIDE
    assert d == D_MODEL and seq % 2048 == 0

    w_in_p = _pack_w_in(w_in)
    w_branch_b = _bf(w_branch)
    w_out_b = _bf(w_out)
    ffn_up_b = _bf(ffn_up)
    ffn_down_b = _bf(ffn_down)
    gla_w_gk_p = _pad_axis(gla_w_gk, 1, LANE)
    lru_conv_w_p = _pad_axis(lru_conv_w, 1, 8)
    ffn_conv_w_p = _pad_axis(ffn_conv_w, 1, 8)
    rwkv_w_lora_p = _bf(_pad_axis(rwkv_w_lora, 1, LANE))
    rwkv_a_lora_p = _bf(_pad_axis(rwkv_a_lora, 1, LANE))
    cmp_pos_p = jnp.broadcast_to(nsa_cmp_pos.reshape(depth, 1, NSA_CMP_LEN * NSA_DH), (depth, 8, NSA_CMP_LEN * NSA_DH))
    toep_t, band, cfar = _bias_tables(rel_bias)
    ovl = _overlap_table(nc, seq)

    tm = min(1024, seq)
    xf = x.reshape(m, d)
    for l in range(depth):
        proj = norm_matmul(xf, _row(attn_norm), w_in_p, l, tm, 512)
        y_a = gla_mixer(proj, gla_w_gk_p, _row(gla_b_gk), _row(gla_out_norm), l, batch, seq)
        y_b = lru_mixer(proj, lru_conv_w_p, _row(lru_conv_b), _bf(lru_w_a), _row(lru_b_a), _bf(lru_w_i), _row(lru_b_i),
                        _row(lru_lambda), l, batch, seq, min(256, seq))
        qn, ks, vs, kw, vw = nsa_prep(proj, _row(nsa_q_norm), _row(nsa_k_norm), l, min(512, seq))
        kgrp = proj[:, P_NSA_KV:P_NSA_KV + NSA_DH].reshape(batch, nc, NSA_CMP_STRIDE * NSA_DH)
        vgrp = proj[:, P_NSA_KV + NSA_DH:P_NSA_KV + 2 * NSA_DH].reshape(batch, nc, NSA_CMP_STRIDE * NSA_DH)
        kc, vc = nsa_compress(kgrp, vgrp, cmp_pos_p, _bf(nsa_cmp_k1), _bf(nsa_cmp_k2), _bf(nsa_cmp_v1),
                              _bf(nsa_cmp_v2), _row(nsa_k_norm), l)
        ocmp, sel = nsa_cmp_attention(cfar, qn, kc, vc, band, ovl, proj, batch, seq)
        seq3 = lambda t: t.reshape(batch, seq, NSA_DH)
        y_c = nsa_sel_win_attention(qn, seq3(ks), seq3(vs), seq3(kw), seq3(vw), sel, toep_t, proj, ocmp, batch, seq)
        y_d = rwkv_mixer(proj, _pack_rwkv_vec(rwkv_mu), _row(rwkv_w0), rwkv_w_lora_p, _row(rwkv_a0), rwkv_a_lora_p,
                         _bf(rwkv_g_lora), _row(rwkv_k_k), _row(rwkv_k_a), _row(rwkv_r_k.reshape(depth, RWKV_WIDTH)),
                         _row(rwkv_ln_w), _row(rwkv_ln_b), l, batch, seq)
        merged = merge_branches((y_a, y_b, y_c, y_d), w_branch_b, proj, l, min(512, seq), 512)
        xf = matmul_residual(merged, w_out_b, xf, l, tm, 512)
        act = ffn_up_conv(xf, _row(ffn_norm), ffn_up_b, ffn_conv_w_p, _row(ffn_conv_b), l, seq, tm, 512)
        xf = matmul_residual(act, ffn_down_b, xf, l, tm, 512)
    return xf.reshape(batch, seq, d)
```

```python
import functools
import math

import numpy as np
import jax
import jax.numpy as jnp
from jax import lax
from jax.experimental import pallas as pl
from jax.experimental.pallas import tpu as pltpu

F32 = jnp.float32
BF16 = jnp.bfloat16
HI = lax.Precision.HIGHEST

LANE = 128
VMEM_LIMIT = 56 * 1024 * 1024

D_MODEL = 2048
N_BRANCH = 4
BRANCH_WIDTH = 512

GLA_HEADS, GLA_DK, GLA_DV, GLA_LOWRANK, GLA_NORMALIZER = 4, 64, 128, 16, 16.0
LRU_WIDTH, LRU_BLOCKS, LRU_CONV, LRU_C = 512, 4, 4, 8.0
NSA_HEADS, NSA_DH = 4, 128
NSA_CMP_LEN, NSA_CMP_STRIDE, NSA_SEL_BLOCK, NSA_SEL_TOPK, NSA_WINDOW, NSA_QBLOCK = 32, 16, 64, 16, 512, 128
RWKV_HEADS, RWKV_DH, RWKV_WIDTH = 8, 64, 512
RWKV_W_LORA, RWKV_A_LORA, RWKV_G_LORA = 96, 96, 256
NUM_BUCKETS, MAX_DISTANCE = 32, 128
D_FF, FFN_CONV = 5632, 3
NEG_BIG = 1e9
MASKED = -1e30
M_INIT = -1e20

P_GATES, P_RWKV, P_LRU_X, P_LRU_G, P_NSA_Q, P_GLA_V, P_GLA_G = 0, 8192, 10240, 10752, 11264, 11776, 12288
P_GLA_Q, P_NSA_KV, P_GLA_K, P_GLA_LR, P_NSA_GATE, P_TOTAL = 12800, 13056, 13824, 14080, 14208, 14336
RW_R, RW_K, RW_V, RW_XW, RW_XA, RW_XG, RW_TOTAL = 0, 512, 1024, 1536, 1664, 1792, 2048

CHUNK = 64


def _cparams(*sem):
    return pltpu.CompilerParams(dimension_semantics=sem, vmem_limit_bytes=VMEM_LIMIT)


def _dot(a, b, prec=None):
    return jnp.dot(a, b, preferred_element_type=F32, precision=prec)


def _dot_nt(a, b, prec=None):
    return lax.dot_general(a, b, (((1,), (1,)), ((), ())), preferred_element_type=F32, precision=prec)


def _dot_tn(a, b, prec=None):
    return lax.dot_general(a, b, (((0,), (0,)), ((), ())), preferred_element_type=F32, precision=prec)


def _bf(x):
    return x.astype(BF16)


def _sigmoid(x):
    return 1.0 / (1.0 + jnp.exp(-x))


def _log_sigmoid(x):
    return jnp.minimum(x, 0.0) - jnp.log(1.0 + jnp.exp(-jnp.abs(x)))


def _gelu_tanh(x):
    return 0.5 * x * (1.0 + jnp.tanh(math.sqrt(2.0 / math.pi) * (x + 0.044715 * (x * x * x))))


def _iota(shape, dim):
    return lax.broadcasted_iota(jnp.int32, shape, dim)


def _idiv(x, d):
    assert d & (d - 1) == 0
    return jnp.right_shift(x, d.bit_length() - 1)


def _shift_rows(x, s, top):
    xr = pltpu.roll(x, s, 0)
    tr = pltpu.roll(top, s, 0)
    row = _iota((8, x.shape[1]), 0)
    head = jnp.where(row < s, tr, xr[:8])
    return jnp.concatenate([head, xr[8:]], axis=0)


def _norm_mm_kernel(x_ref, g_ref, w_ref, o_ref, xn_ref):
    @pl.when(pl.program_id(1) == 0)
    def _():
        x = x_ref[...]
        ms = jnp.mean(x * x, axis=-1, keepdims=True)
        xn_ref[...] = _bf(x * lax.rsqrt(ms + 1e-6) * g_ref[...])

    o_ref[...] = _dot(xn_ref[...], w_ref[...]).astype(o_ref.dtype)


def norm_matmul(x, gain, w, layer, tm, tn):
    m, k = x.shape
    n = w.shape[2]
    return pl.pallas_call(
        _norm_mm_kernel,
        grid=(m // tm, n // tn),
        in_specs=[pl.BlockSpec((tm, k), lambda i, j: (i, 0)),
                  pl.BlockSpec((None, 1, k), lambda i, j: (layer, 0, 0)),
                  pl.BlockSpec((None, k, tn), lambda i, j: (layer, 0, j))],
        out_specs=pl.BlockSpec((tm, tn), lambda i, j: (i, j)),
        out_shape=jax.ShapeDtypeStruct((m, n), F32),
        scratch_shapes=[pltpu.VMEM((tm, k), BF16)],
        compiler_params=_cparams("parallel", "arbitrary"),
        name="norm_matmul",
    )(x, gain, w)


def _mm_res_kernel(a_ref, w_ref, r_ref, o_ref):
    o_ref[...] = r_ref[...] + _dot(a_ref[...], w_ref[...])


def matmul_residual(a, w, res, layer, tm, tn):
    m, k = a.shape
    n = w.shape[2]
    return pl.pallas_call(
        _mm_res_kernel,
        grid=(m // tm, n // tn),
        in_specs=[pl.BlockSpec((tm, k), lambda i, j: (i, 0)),
                  pl.BlockSpec((None, k, tn), lambda i, j: (layer, 0, j)),
                  pl.BlockSpec((tm, tn), lambda i, j: (i, j))],
        out_specs=pl.BlockSpec((tm, tn), lambda i, j: (i, j)),
        out_shape=jax.ShapeDtypeStruct((m, n), F32),
        compiler_params=_cparams("parallel", "arbitrary"),
        name="matmul_residual",
    )(a, w, res)


def _merge_kernel(ya_ref, yb_ref, yc_ref, yd_ref, wb_ref, g0_ref, g1_ref, g2_ref, g3_ref, o_ref):
    acc = None
    for n, (y_ref, g_ref) in enumerate(((ya_ref, g0_ref), (yb_ref, g1_ref), (yc_ref, g2_ref), (yd_ref, g3_ref))):
        t = _sigmoid(g_ref[...]) * _dot(y_ref[...], wb_ref[n])
        acc = t if acc is None else acc + t
    o_ref[...] = _bf(acc)


def merge_branches(ys, w_branch, proj, layer, tm, tn):
    m = ys[0].shape[0]
    nj = D_MODEL // tn
    y_spec = pl.BlockSpec((tm, BRANCH_WIDTH), lambda i, j: (i, 0))
    gate_specs = [pl.BlockSpec((tm, tn), functools.partial(lambda i, j, n: (i, (P_GATES + n * D_MODEL) // tn + j), n=n))
                  for n in range(N_BRANCH)]
    return pl.pallas_call(
        _merge_kernel,
        grid=(m // tm, nj),
        in_specs=[y_spec] * 4 + [pl.BlockSpec((None, N_BRANCH, BRANCH_WIDTH, tn), lambda i, j: (layer, 0, 0, j))]
        + gate_specs,
        out_specs=pl.BlockSpec((tm, tn), lambda i, j: (i, j)),
        out_shape=jax.ShapeDtypeStruct((m, D_MODEL), BF16),
        compiler_params=_cparams("parallel", "arbitrary"),
        name="merge_branches",
    )(*ys, w_branch, proj, proj, proj, proj)


def _ffn_up_kernel(x_ref, g_ref, wg_ref, wv_ref, cw_g_ref, cw_v_ref, cb_g_ref, cb_v_ref, o_ref,
                   xn_ref, carry_ref, *, tiles_per_seq):
    i, j = pl.program_id(0), pl.program_id(1)

    @pl.when(j == 0)
    def _():
        x = x_ref[...]
        ms = jnp.mean(x * x, axis=-1, keepdims=True)
        xn_ref[...] = _bf(x * lax.rsqrt(ms + 1e-6) * g_ref[...])

    first = (i % tiles_per_seq) == 0

    @pl.when(first)
    def _():
        carry_ref[j] = jnp.zeros(carry_ref.shape[1:], F32)

    xn = xn_ref[...]
    outs = []
    for half, (w_ref, cw_ref, cb_ref) in enumerate(((wg_ref, cw_g_ref, cb_g_ref), (wv_ref, cw_v_ref, cb_v_ref))):
        u = _dot(xn, w_ref[...])
        top = carry_ref[j, half]
        carry_ref[j, half] = u[u.shape[0] - 8:]
        cw = cw_ref[...]
        outs.append(cw[0:1] * _shift_rows(u, 2, top) + cw[1:2] * _shift_rows(u, 1, top) + cw[2:3] * u + cb_ref[...])
    gate, val = outs
    o_ref[...] = _bf(gate * _sigmoid(gate) * val)


def ffn_up_conv(x, gain, w_up, conv_w, conv_b, layer, seq, tm, tn):
    m, k = x.shape
    nj = D_FF // tn
    kern = functools.partial(_ffn_up_kernel, tiles_per_seq=seq // tm)
    return pl.pallas_call(
        kern,
        grid=(m // tm, nj),
        in_specs=[pl.BlockSpec((tm, k), lambda i, j: (i, 0)),
                  pl.BlockSpec((None, 1, k), lambda i, j: (layer, 0, 0)),
                  pl.BlockSpec((None, k, tn), lambda i, j: (layer, 0, j)),
                  pl.BlockSpec((None, k, tn), lambda i, j: (layer, 0, nj + j)),
                  pl.BlockSpec((None, 8, tn), lambda i, j: (layer, 0, j)),
                  pl.BlockSpec((None, 8, tn), lambda i, j: (layer, 0, nj + j)),
                  pl.BlockSpec((None, 1, tn), lambda i, j: (layer, 0, j)),
                  pl.BlockSpec((None, 1, tn), lambda i, j: (layer, 0, nj + j))],
        out_specs=pl.BlockSpec((tm, tn), lambda i, j: (i, j)),
        out_shape=jax.ShapeDtypeStruct((m, D_FF), BF16),
        scratch_shapes=[pltpu.VMEM((tm, k), BF16), pltpu.VMEM((nj, 2, 8, tn), F32)],
        compiler_params=_cparams("arbitrary", "arbitrary"),
        name="ffn_up",
    )(x, gain, w_up, w_up, conv_w, conv_w, conv_b, conv_b)


def _gla_kernel(q_ref, k_ref, v_ref, g_ref, lr_ref, wgk_ref, bgk_ref, gain_ref, o_ref, st_ref):
    c = CHUNK

    nb = q_ref.shape[0]
    npair = GLA_HEADS // 2

    @pl.when(pl.program_id(0) == 0)
    def _():
        st_ref[...] = jnp.zeros_like(st_ref)

    row = _iota((c, c), 0)
    col = _iota((c, c), 1)
    tri_incl = (col <= row).astype(F32)
    lane = _iota((1, LANE), 1)
    bd = _idiv(_iota((2 * GLA_DV, LANE), 0), GLA_DV) == _idiv(_iota((2 * GLA_DV, LANE), 1), GLA_DK)
    sub = 16
    nsub = c // sub
    scores = {}
    o_inter = {}
    vbs = []
    for bi in range(nb):
        log_a = _log_sigmoid(_dot(lr_ref[bi], wgk_ref[...], HI) + bgk_ref[...]) * (1.0 / GLA_NORMALIZER)
        b = _dot(tri_incl, log_a, HI)
        vb = _bf(v_ref[bi])
        vbs.append(vb)
        for p in range(npair):
            sl = slice(p * LANE, (p + 1) * LANE)
            qp = q_ref[bi, :, sl] * (GLA_DK ** -0.5)
            kp = k_ref[bi, :, sl]
            bp = b[:, sl]
            b_last = bp[c - 1:c]
            st = st_ref[bi * npair + p]
            o_inter[bi, p] = _dot_nt(_bf(qp * jnp.exp(bp)), _bf(st))
            upd = _dot_tn(vb[:, 2 * p * GLA_DV:(2 * p + 2) * GLA_DV], _bf(kp * jnp.exp(b_last - bp)))
            st_ref[bi * npair + p] = st * jnp.exp(b_last) + jnp.where(bd, upd, 0.0)
            for i in range(nsub):
                rows = slice(i * sub, (i + 1) * sub)
                nrow = (i + 1) * sub
                bref = bp[i * sub - 1:i * sub] if i > 0 else jnp.zeros((1, LANE), F32)
                qi = qp[rows] * jnp.exp(bp[rows] - bref)
                ki = _bf(kp[:nrow] * jnp.exp(jnp.minimum(bref - bp[:nrow], 80.0)))
                causal = _iota((sub, nrow), 1) <= (_iota((sub, nrow), 0) + i * sub)
                for hh in range(2):
                    head_lanes = _idiv(lane, GLA_DK) == hh
                    s = _dot_nt(_bf(jnp.where(head_lanes, qi, 0.0)), ki)
                    scores[bi, 2 * p + hh, i] = _bf(jnp.where(causal, s, 0.0))
    gain = gain_ref[...]
    for bi in range(nb):
        for h in range(GLA_HEADS):
            hs = slice(h * GLA_DV, (h + 1) * GLA_DV)
            intra = [_dot(scores[bi, h, i], vbs[bi][:(i + 1) * sub, hs]) for i in range(nsub)]
            hh = h % 2
            o = o_inter[bi, h // 2][:, hh * GLA_DV:(hh + 1) * GLA_DV] + jnp.concatenate(intra, axis=0)
            y = o * lax.rsqrt(jnp.mean(o * o, axis=-1, keepdims=True) + 1e-6) * gain
            g = g_ref[bi, :, hs]
            o_ref[bi, :, hs] = _bf(y * (g * _sigmoid(g)))


def gla_mixer(proj, w_gk, b_gk, out_gain, layer, batch, seq):
    nt = seq // CHUNK
    def col(off, width):
        return pl.BlockSpec((batch, CHUNK, width), lambda t: (0, t, off // width))
    proj3 = proj.reshape(batch, seq, proj.shape[1])
    out = pl.pallas_call(
        _gla_kernel,
        grid=(nt,),
        in_specs=[col(P_GLA_Q, 256), col(P_GLA_K, 256), col(P_GLA_V, 512), col(P_GLA_G, 512), col(P_GLA_LR, 128),
                  pl.BlockSpec((None, LANE, 256), lambda t: (layer, 0, 0)),
                  pl.BlockSpec((None, 1, 256), lambda t: (layer, 0, 0)),
                  pl.BlockSpec((None, 1, GLA_DV), lambda t: (layer, 0, 0))],
        out_specs=pl.BlockSpec((batch, CHUNK, 512), lambda t: (0, t, 0)),
        out_shape=jax.ShapeDtypeStruct((batch, seq, 512), BF16),
        scratch_shapes=[pltpu.VMEM((batch * (GLA_HEADS // 2), 2 * GLA_DV, LANE), F32)],
        compiler_params=_cparams("arbitrary"),
        name="gla_mixer",
    )(proj3, proj3, proj3, proj3, proj3, w_gk, b_gk, out_gain)
    return out.reshape(batch * seq, 512)


def _lru_kernel(x_ref, gb_ref, cw_ref, cb_ref, wa_ref, ba_ref, wi_ref, bi_ref, lam_ref, o_ref, xc_ref, h_ref):
    t = x_ref.shape[0]

    @pl.when(pl.program_id(1) == 0)
    def _():
        xc_ref[...] = jnp.zeros_like(xc_ref)
        h_ref[...] = jnp.zeros_like(h_ref)

    x = x_ref[...]
    top = xc_ref[...]
    xc_ref[...] = x[t - 8:]
    cw = cw_ref[...]
    xc = (cw[0:1] * _shift_rows(x, 3, top) + cw[1:2] * _shift_rows(x, 2, top) + cw[2:3] * _shift_rows(x, 1, top)
          + cw[3:4] * x + cb_ref[...])
    xcb = _bf(xc)
    ra, ri = [], []
    for n in range(LRU_BLOCKS):
        blk = xcb[:, n * LANE:(n + 1) * LANE]
        ra.append(_dot(blk, wa_ref[n]))
        ri.append(_dot(blk, wi_ref[n]))
    r = _sigmoid(jnp.concatenate(ra, axis=1) + ba_ref[...])
    gi = _sigmoid(jnp.concatenate(ri, axis=1) + bi_ref[...])
    log_a = LRU_C * r * _log_sigmoid(lam_ref[...])
    a = jnp.exp(log_a)
    u = jnp.sqrt(-jnp.tanh(log_a) * (a * a + 1.0)) * (gi * xc)
    row = _iota((t, LRU_WIDTH), 0)
    k = 1
    while k < t:
        a_sh = jnp.where(row < k, 1.0, pltpu.roll(a, k, 0))
        u_sh = jnp.where(row < k, 0.0, pltpu.roll(u, k, 0))
        u = u + a * u_sh
        a = a * a_sh
        k *= 2
    h = a * h_ref[0:1] + u
    h_ref[...] = jnp.broadcast_to(h[t - 1:t], h_ref.shape)
    o_ref[...] = _bf(h * _gelu_tanh(gb_ref[...]))


def lru_mixer(proj, conv_w, conv_b, w_a, b_a, w_i, b_i, lam, layer, batch, seq, tt):
    nt = seq // tt
    vec = pl.BlockSpec((None, 1, LRU_WIDTH), lambda b, t: (layer, 0, 0))
    wblk = pl.BlockSpec((None, LRU_BLOCKS, LANE, LANE), lambda b, t: (layer, 0, 0, 0))
    return pl.pallas_call(
        _lru_kernel,
        grid=(batch, nt),
        in_specs=[pl.BlockSpec((tt, 512), lambda b, t: (b * nt + t, P_LRU_X // 512)),
                  pl.BlockSpec((tt, 512), lambda b, t: (b * nt + t, P_LRU_G // 512)),
                  pl.BlockSpec((None, 8, LRU_WIDTH), lambda b, t: (layer, 0, 0)), vec, wblk, vec, wblk, vec, vec],
        out_specs=pl.BlockSpec((tt, 512), lambda b, t: (b * nt + t, 0)),
        out_shape=jax.ShapeDtypeStruct((batch * seq, 512), BF16),
        scratch_shapes=[pltpu.VMEM((8, LRU_WIDTH), F32), pltpu.VMEM((8, LRU_WIDTH), F32)],
        compiler_params=_cparams("parallel", "arbitrary"),
        name="lru_mixer",
    )(proj, proj, conv_w, conv_b, w_a, b_a, w_i, b_i, lam)


def _rwkv_kernel(f_ref, mu_ref, w0_ref, wl_ref, a0_ref, al_ref, gl_ref, kk_ref, ka_ref, rk_ref, lnw_ref, lnb_ref,
                 o_ref, st_ref, prev_ref):
    c = CHUNK
    n = RWKV_DH

    nb = f_ref.shape[0]

    @pl.when(pl.program_id(0) == 0)
    def _():
        st_ref[...] = jnp.zeros_like(st_ref)
        prev_ref[...] = jnp.zeros_like(prev_ref)

    feats = [f_ref[b] for b in range(nb)]
    prev = jnp.concatenate([_shift_rows(feats[b], 1, prev_ref[b]) for b in range(nb)], axis=0)
    for b in range(nb):
        prev_ref[b] = feats[b][c - 8:]
    feat = jnp.concatenate(feats, axis=0)
    xm = feat + (prev - feat) * mu_ref[...]
    r = xm[:, RW_R:RW_R + 512]
    k = xm[:, RW_K:RW_K + 512]
    v = xm[:, RW_V:RW_V + 512]
    log_w = -math.exp(-0.5) * _sigmoid(w0_ref[...] + _dot(_bf(jnp.tanh(xm[:, RW_XW:RW_XW + LANE])), wl_ref[...]))
    a = _sigmoid(a0_ref[...] + _dot(_bf(xm[:, RW_XA:RW_XA + LANE]), al_ref[...]))
    g = _dot(_bf(_sigmoid(xm[:, RW_XG:RW_XG + 256])), gl_ref[...])

    assert c == n and 2 * n == LANE
    head0 = _iota((1, LANE), 1) < n
    same_head = _idiv(_iota((LANE, LANE), 0), n) == _idiv(_iota((LANE, LANE), 1), n)
    head_ones = _bf(same_head.astype(F32))
    row = _iota((c, LANE), 0)
    col = jnp.bitwise_and(_iota((c, LANE), 1), n - 1)
    tri_incl = col <= row
    tri_strict = col < row
    rr_ = _iota((nb * c, nb * c), 0)
    cc_ = _iota((nb * c, nb * c), 1)
    tri_sq = _bf(((cc_ <= rr_) & (_idiv(cc_, c) == _idiv(rr_, c))).astype(F32))

    def stack(t):
        return jnp.concatenate([jnp.where(head0, t, 0.0), jnp.where(head0, 0.0, t)], axis=0)

    def seg_sum(t):
        hi = _bf(t)
        return _dot(hi, head_ones) + _dot(_bf(t - hi.astype(F32)), head_ones)

    lw_hi = _bf(log_w)
    lw_r = log_w - lw_hi.astype(F32)
    lw_mid = _bf(lw_r)
    cum_all = _dot(tri_sq, lw_hi) + _dot(tri_sq, lw_mid) + _dot(tri_sq, _bf(lw_r - lw_mid.astype(F32)))

    pairs = range(RWKV_HEADS // 2)
    sls = [slice(p * LANE, (p + 1) * LANE) for p in pairs]
    kk = [k[:, sl] * kk_ref[:, sl] for sl in sls]
    ss = [seg_sum(t * t) for t in kk]
    kk = [t / jnp.maximum(jnp.sqrt(s), 1e-12) for t, s in zip(kk, ss)]
    k2 = [k[:, sl] * (1.0 + (a[:, sl] - 1.0) * ka_ref[:, sl]) for sl in sls]
    e_neg = [jnp.exp(-cum_all[:, sl]) for sl in sls]
    rt_all = [_bf(r[:, sl] * jnp.exp(cum_all[:, sl])) for sl in sls]
    kt_all = [_bf(kk[p] * jnp.exp(cum_all[:, sls[p]] - log_w[:, sls[p]])) for p in pairs]
    kb_all = [k2[p] * e_neg[p] for p in pairs]
    bb_all = [kk[p] * a[:, sls[p]] * e_neg[p] for p in pairs]
    chains = [(b, p) for b in range(nb) for p in pairs]
    nch = range(len(chains))
    rows = [slice(b * c, (b + 1) * c) for b, _ in chains]
    e_last = [jnp.exp(cum_all[b * c + c - 1:(b + 1) * c, sls[p]]) for b, p in chains]
    rt = [rt_all[p][rows[i]] for i, (_, p) in enumerate(chains)]
    kt = [kt_all[p][rows[i]] for i, (_, p) in enumerate(chains)]
    kb = [kb_all[p][rows[i]] for i, (_, p) in enumerate(chains)]
    bb = [bb_all[p][rows[i]] for i, (_, p) in enumerate(chains)]
    vv = [v[rows[i], sls[p]] for i, (_, p) in enumerate(chains)]
    kb2 = [_bf(stack(t)) for t in kb]
    bb2 = [_bf(stack(t)) for t in bb]
    vstk = [_bf(stack(t)) for t in vv]
    st = [st_ref[i] for i in nch]
    stb = [_bf(t) for t in st]
    a_mat = [jnp.where(tri_strict, _dot_nt(kt[i], bb2[i]), 0.0) for i in nch]
    b_mat = [jnp.where(tri_strict, _dot_nt(kt[i], kb2[i]), 0.0) for i in nch]
    rr_mat = [jnp.where(tri_incl, _dot_nt(rt[i], kb2[i]), 0.0) for i in nch]
    rb_mat = [jnp.where(tri_incl, _dot_nt(rt[i], bb2[i]), 0.0) for i in nch]
    rhs = [_dot_nt(kt[i], stb[i]) + _dot(_bf(b_mat[i]), vstk[i]) for i in nch]
    y0 = [_dot_nt(rt[i], stb[i]) + _dot(_bf(rr_mat[i]), vstk[i]) for i in nch]
    nmat = [-t for t in a_mat]
    tm1 = list(nmat)
    pw = 1
    while 2 * pw < c:
        nmat = [_dot(_bf(t), _bf(stack(t))) for t in nmat]
        tm1 = [tm1[i] + nmat[i] + _dot(_bf(tm1[i]), _bf(stack(nmat[i]))) for i in nch]
        pw *= 2
    u = [rhs[i] + _dot(_bf(tm1[i]), _bf(stack(rhs[i]))) for i in nch]
    y = [y0[i] - _dot(_bf(rb_mat[i]), _bf(stack(u[i]))) for i in nch]
    upd = [_dot_tn(_bf(jnp.concatenate([vv[i], -u[i]], axis=0)),
                   _bf(jnp.concatenate([kb[i], bb[i]], axis=0) * e_last[i])) for i in nch]
    for i in nch:
        st_ref[i] = st[i] * e_last[i] + jnp.where(same_head, upd[i], 0.0)
    y_all = [jnp.concatenate([y[b * len(pairs) + p] for b in range(nb)], axis=0) for p in pairs]
    yc = [y_all[p] - seg_sum(y_all[p]) * (1.0 / n) for p in pairs]
    var = [seg_sum(t * t) * (1.0 / n) for t in yc]
    bonus = [seg_sum(r[:, sls[p]] * k2[p] * rk_ref[:, sls[p]]) for p in pairs]
    out = [(yc[p] * lax.rsqrt(var[p] + 64e-5) * lnw_ref[:, sls[p]] + lnb_ref[:, sls[p]] + bonus[p] * v[:, sls[p]])
           for p in pairs]
    out = _bf(jnp.concatenate(out, axis=1) * g)
    for b in range(nb):
        o_ref[b] = out[b * c:(b + 1) * c]


def rwkv_mixer(proj, mu, w0, w_lora, a0, a_lora, g_lora, k_k, k_a, r_k, ln_w, ln_b, layer, batch, seq):
    nt = seq // CHUNK
    vec = pl.BlockSpec((None, 1, RWKV_WIDTH), lambda t: (layer, 0, 0))
    out = pl.pallas_call(
        _rwkv_kernel,
        grid=(nt,),
        in_specs=[pl.BlockSpec((batch, CHUNK, RW_TOTAL), lambda t: (0, t, P_RWKV // RW_TOTAL)),
                  pl.BlockSpec((None, 1, RW_TOTAL), lambda t: (layer, 0, 0)),
                  vec, pl.BlockSpec((None, LANE, RWKV_WIDTH), lambda t: (layer, 0, 0)),
                  vec, pl.BlockSpec((None, LANE, RWKV_WIDTH), lambda t: (layer, 0, 0)),
                  pl.BlockSpec((None, RWKV_G_LORA, RWKV_WIDTH), lambda t: (layer, 0, 0)),
                  vec, vec, vec, vec, vec],
        out_specs=pl.BlockSpec((batch, CHUNK, 512), lambda t: (0, t, 0)),
        out_shape=jax.ShapeDtypeStruct((batch, seq, 512), BF16),
        scratch_shapes=[pltpu.VMEM((batch * (RWKV_HEADS // 2), LANE, LANE), F32),
                        pltpu.VMEM((batch, 8, RW_TOTAL), F32)],
        compiler_params=_cparams("arbitrary"),
        name="rwkv_mixer",
    )(proj.reshape(batch, seq, proj.shape[1]), mu, w0, w_lora, a0, a_lora, g_lora, k_k, k_a, r_k, ln_w, ln_b)
    return out.reshape(batch * seq, 512)


def _head_rms(x, gain):
    return x * lax.rsqrt(jnp.mean(x * x, axis=-1, keepdims=True) + 1e-6) * gain


def _nsa_prep_kernel(q_ref, kv_ref, qg_ref, kg_ref, qn_ref, ks_ref, vs_ref, kw_ref, vw_ref):
    qg = qg_ref[...] * (NSA_DH ** -0.5)
    for h in range(NSA_HEADS):
        sl = slice(h * NSA_DH, (h + 1) * NSA_DH)
        qn_ref[:, sl] = _bf(_head_rms(q_ref[:, sl], qg))
    kg = kg_ref[...]
    ks_ref[...] = _bf(_head_rms(kv_ref[:, 2 * NSA_DH:3 * NSA_DH], kg))
    vs_ref[...] = _bf(kv_ref[:, 3 * NSA_DH:4 * NSA_DH])
    kw_ref[...] = _bf(_head_rms(kv_ref[:, 4 * NSA_DH:5 * NSA_DH], kg))
    vw_ref[...] = _bf(kv_ref[:, 5 * NSA_DH:6 * NSA_DH])


def nsa_prep(proj, q_gain, k_gain, layer, tt):
    m = proj.shape[0]
    gain = pl.BlockSpec((None, 1, NSA_DH), lambda i: (layer, 0, 0))
    kv_out = pl.BlockSpec((tt, NSA_DH), lambda i: (i, 0))
    kv_shape = jax.ShapeDtypeStruct((m, NSA_DH), BF16)
    return pl.pallas_call(
        _nsa_prep_kernel,
        grid=(m // tt,),
        in_specs=[pl.BlockSpec((tt, 512), lambda i: (i, P_NSA_Q // 512)),
                  pl.BlockSpec((tt, 768), lambda i: (i, P_NSA_KV // 768)), gain, gain],
        out_specs=[pl.BlockSpec((tt, 512), lambda i: (i, 0)), kv_out, kv_out, kv_out, kv_out],
        out_shape=[jax.ShapeDtypeStruct((m, 512), BF16), kv_shape, kv_shape, kv_shape, kv_shape],
        compiler_params=_cparams("parallel"),
        name="nsa_prep",
    )(proj, proj, q_gain, k_gain)


def _nsa_compress_kernel(kg_ref, vg_ref, pos_ref, k1_ref, k2_ref, v1_ref, v2_ref, gain_ref, kc_ref, vc_ref):
    nc = kg_ref.shape[0]
    half = NSA_CMP_STRIDE * NSA_DH
    pos = _bf(pos_ref[...])

    def compress(g_ref, w1_ref, w2_ref):
        grp = _bf(g_ref[...])
        first = _dot(grp, w1_ref[:half])
        second = _dot(grp, w1_ref[half:])
        const = _dot(pos, w1_ref[...])[0:1]
        hid = first + pltpu.roll(second, nc - 1, 0) + const
        return _dot(_bf(_gelu_tanh(hid)), w2_ref[...])

    kc_ref[...] = _bf(_head_rms(compress(kg_ref, k1_ref, k2_ref), gain_ref[...]))
    vc_ref[...] = _bf(compress(vg_ref, v1_ref, v2_ref))


def nsa_compress(kgrp, vgrp, pos, k1, k2, v1, v2, k_gain, layer):
    batch, nc, width = kgrp.shape
    grp = pl.BlockSpec((None, nc, width), lambda b: (b, 0, 0))
    w1 = pl.BlockSpec((None, 2 * width, NSA_DH), lambda b: (layer, 0, 0))
    w2 = pl.BlockSpec((None, NSA_DH, NSA_DH), lambda b: (layer, 0, 0))
    out = pl.BlockSpec((None, nc, NSA_DH), lambda b: (b, 0, 0))
    shape = jax.ShapeDtypeStruct((batch, nc, NSA_DH), BF16)
    return pl.pallas_call(
        _nsa_compress_kernel,
        grid=(batch,),
        in_specs=[grp, grp, pl.BlockSpec((None, 8, 2 * width), lambda b: (layer, 0, 0)), w1, w2, w1, w2,
                  pl.BlockSpec((None, 1, NSA_DH), lambda b: (layer, 0, 0))],
        out_specs=[out, out],
        out_shape=[shape, shape],
        compiler_params=_cparams("parallel"),
        name="nsa_compress",
    )(kgrp, vgrp, pos, k1, k2, v1, v2, k_gain)


CMP_GROUP = 4


def _nsa_cmp_kernel(cfar_ref, q_ref, kc_ref, vc_ref, *rest):
    band_refs = rest[:CMP_GROUP]
    ovl_ref, gate_ref, o_ref, sel_ref = rest[CMP_GROUP:]
    qb = NSA_QBLOCK
    nc = kc_ref.shape[0]
    kc = kc_ref[...]
    vc = vc_ref[...]
    ovl = ovl_ref[...]
    r = _iota((qb, nc), 0)
    ncol = _iota((qb, nc), 1)
    blk = _iota((qb, LANE), 1)
    works = []
    for g in range(CMP_GROUP):
        i = pl.program_id(1) * CMP_GROUP + g
        rows = slice(g * qb, (g + 1) * qb)
        dist = qb * i + r - NSA_CMP_STRIDE * ncol - (NSA_CMP_LEN - 1)
        visible = dist >= 0
        lo = (qb // NSA_CMP_STRIDE) * i - 9
        in_band = (ncol >= lo) & (ncol <= lo + 15)
        gate = gate_ref[rows, :]
        p_sum = jnp.zeros((qb, nc), F32)
        for h in range(NSA_HEADS):
            sl = slice(h * NSA_DH, (h + 1) * NSA_DH)
            band = jnp.concatenate([band_refs[g][h]] * (nc // LANE), axis=1)
            logit = _dot_nt(q_ref[rows, sl], kc) + jnp.where(in_band, band, cfar_ref[h])
            logit = jnp.where(visible, logit, MASKED)
            mx = jnp.max(logit, axis=-1, keepdims=True)
            p = jnp.where(visible, jnp.exp(logit - mx), 0.0)
            p = p / jnp.maximum(jnp.sum(p, axis=-1, keepdims=True), 1e-30)
            p_sum = p_sum + p
            o_ref[rows, sl] = _sigmoid(gate[:, h:h + 1]) * _dot(_bf(p), vc)
        p_hi = _bf(p_sum)
        score = _dot(p_hi, ovl) + _dot(_bf(p_sum - p_hi.astype(F32)), ovl)
        pos = qb * i + _iota((qb, LANE), 0)
        cur = _idiv(pos, NSA_SEL_BLOCK)
        forced = (blk == 0) | (blk == cur) | (blk == cur - 1)
        works.append(jnp.where(forced, NEG_BIG, jnp.where(blk * NSA_SEL_BLOCK <= pos, score, -NEG_BIG)))
    work = jnp.concatenate(works, axis=0)
    sel = jnp.zeros(work.shape, F32)
    blk_f = _iota(work.shape, 1).astype(F32)
    for _ in range(NSA_SEL_TOPK):
        mx = jnp.max(work, axis=-1, keepdims=True)
        first = jnp.min(jnp.where(work == mx, blk_f, float(LANE)), axis=-1, keepdims=True)
        pick = blk_f == first
        sel = jnp.where(pick, 1.0, sel)
        work = jnp.where(pick, -jnp.inf, work)
    sel_ref[...] = _bf(sel)


def nsa_cmp_attention(cfar, qn, kc, vc, band, ovl, proj, batch, seq):
    rows = CMP_GROUP * NSA_QBLOCK
    ns = seq // rows
    nc = kc.shape[1]
    full = pl.BlockSpec((None, nc, NSA_DH), lambda b, i: (b, 0, 0))
    band_specs = [pl.BlockSpec((None, NSA_HEADS, NSA_QBLOCK, LANE),
                               functools.partial(lambda b, i, g: ((i * CMP_GROUP + g) % 16, 0, 0, 0), g=g))
                  for g in range(CMP_GROUP)]
    return pl.pallas_call(
        _nsa_cmp_kernel,
        grid=(batch, ns),
        in_specs=[pl.BlockSpec(memory_space=pltpu.SMEM),
                  pl.BlockSpec((rows, 512), lambda b, i: (b * ns + i, 0)), full, full] + band_specs
        + [pl.BlockSpec((nc, LANE), lambda b, i: (0, 0)),
           pl.BlockSpec((rows, LANE), lambda b, i: (b * ns + i, P_NSA_GATE // LANE))],
        out_specs=[pl.BlockSpec((rows, 512), lambda b, i: (b * ns + i, 0)),
                   pl.BlockSpec((rows, LANE), lambda b, i: (b * ns + i, 0))],
        out_shape=[jax.ShapeDtypeStruct((batch * seq, 512), F32), jax.ShapeDtypeStruct((batch * seq, LANE), BF16)],
        compiler_params=_cparams("parallel", "arbitrary"),
        name="nsa_cmp_attention",
    )(cfar, qn, kc, vc, *([band] * CMP_GROUP), ovl, proj)


FAR_GROUP = 4
SEL_PAD_TILES = FAR_GROUP
WIN_PAD_TILES = NSA_WINDOW // NSA_QBLOCK


def _nsa_sel_win_kernel(q_ref, ks_ref, vs_ref, kw_ref, vw_ref, sel_ref, near_ref, win_ref, gate_ref, ocmp_ref,
                        o_ref, selt_ref, m_ref, l_ref, acc_ref):
    qb = NSA_QBLOCK
    nh = NSA_HEADS
    i = pl.program_id(1)
    qt = jnp.concatenate([_bf(q_ref[:, h * NSA_DH:(h + 1) * NSA_DH].astype(F32).T) for h in range(nh)], axis=1)
    selt = sel_ref[...].astype(F32).T
    selt_ref[0:8, :] = jnp.full((8, nh * qb), MASKED, F32)
    selt_ref[8:, :] = jnp.concatenate([jnp.where(selt > 0.5, 0.0, MASKED)] * nh, axis=1)
    pad_rows = 8 - 2 * SEL_PAD_TILES

    def reset():
        m_ref[...] = jnp.full_like(m_ref, M_INIT)
        l_ref[...] = jnp.zeros_like(l_ref)
        acc_ref[...] = jnp.zeros_like(acc_ref)

    def attend(k_tile, v_tile, add):
        s = _dot(k_tile, qt)
        if add is not None:
            s = s + add
        m_old = m_ref[...]
        m_new = jnp.maximum(m_old, jnp.max(s, axis=0, keepdims=True))
        p = jnp.exp(s - m_new)
        alpha = jnp.exp(m_old - m_new)
        l_ref[...] = alpha * l_ref[...] + jnp.sum(p, axis=0, keepdims=True)
        acc_ref[...] = alpha * acc_ref[...] + _dot_tn(v_tile, _bf(p))
        m_ref[...] = m_new

    def result():
        return acc_ref[...] / jnp.maximum(l_ref[...], 1e-30)

    def chosen(pt, ntile):
        first = pad_rows + 2 * pt
        rows = [jnp.broadcast_to(selt_ref[pl.ds(first + j, 1), :], (NSA_SEL_BLOCK, nh * qb)) for j in range(2 * ntile)]
        return jnp.concatenate(rows, axis=0)

    def tile(ref, pt, ntile=1):
        return ref[pl.ds(pl.multiple_of(pt * qb, qb), ntile * qb), :]

    reset()
    near = i + SEL_PAD_TILES - 2
    ngroup = (jnp.maximum(i - 2, 0) + FAR_GROUP - 1) // FAR_GROUP

    def far_body(g, carry):
        half = FAR_GROUP // 2
        pa = near - FAR_GROUP * (g + 1)
        pb = pa + half
        sa = _dot(tile(ks_ref, pa, half), qt) + chosen(pa, half)
        sb = _dot(tile(ks_ref, pb, half), qt) + chosen(pb, half)
        m0 = m_ref[...]
        ma = jnp.maximum(m0, jnp.max(sa, axis=0, keepdims=True))
        p_a = jnp.exp(sa - ma)
        pv_a = _dot_tn(tile(vs_ref, pa, half), _bf(p_a))
        mb = jnp.maximum(ma, jnp.max(sb, axis=0, keepdims=True))
        p_b = jnp.exp(sb - mb)
        pv_b = _dot_tn(tile(vs_ref, pb, half), _bf(p_b))
        al_a = jnp.exp(m0 - ma)
        al_b = jnp.exp(ma - mb)
        l_ref[...] = al_b * (al_a * l_ref[...] + jnp.sum(p_a, axis=0, keepdims=True)) + jnp.sum(p_b, axis=0, keepdims=True)
        acc_ref[...] = al_b * (al_a * acc_ref[...] + pv_a) + pv_b
        m_ref[...] = mb
        return carry

    lax.fori_loop(0, ngroup, far_body, 0)
    attend(tile(ks_ref, near, 3), tile(vs_ref, near, 3), chosen(near, 3) + near_ref[...])
    o_sel = result()

    reset()
    nw = NSA_WINDOW // qb
    adds = []
    for d in range(nw + 1):
        add = win_ref[d * qb:(d + 1) * qb, :]
        if d < nw:
            add = add + jnp.where(i - nw + d >= 0, 0.0, MASKED)
        adds.append(add)
    attend(tile(kw_ref, i, nw + 1), tile(vw_ref, i, nw + 1), jnp.concatenate(adds, axis=0))
    o_win = result()

    gate = gate_ref[...]
    for h in range(nh):
        sl = slice(h * NSA_DH, (h + 1) * NSA_DH)
        g_sel = _sigmoid(gate[:, nh + h:nh + h + 1])
        g_win = _sigmoid(gate[:, 2 * nh + h:2 * nh + h + 1])
        o_ref[:, sl] = _bf(ocmp_ref[:, sl] + g_sel * o_sel[:, sl].T + g_win * o_win[:, sl].T)


def nsa_sel_win_attention(qn, ks, vs, kw, vw, sel, near_add, win_add, proj, ocmp, batch, seq):
    nq = seq // NSA_QBLOCK
    wide = NSA_HEADS * NSA_QBLOCK
    padded = lambda t: pl.BlockSpec((None, t.shape[1], NSA_DH), lambda b, i: (b, 0, 0))
    table = lambda t: pl.BlockSpec(t.shape, lambda b, i: (0, 0))
    rows = lambda width, cb=0: pl.BlockSpec((NSA_QBLOCK, width), lambda b, i: (b * nq + i, cb))
    return pl.pallas_call(
        _nsa_sel_win_kernel,
        grid=(batch, nq),
        in_specs=[rows(512), padded(ks), padded(vs), padded(kw), padded(vw), rows(LANE), table(near_add),
                  table(win_add), rows(LANE, P_NSA_GATE // LANE), rows(512)],
        out_specs=rows(512),
        out_shape=jax.ShapeDtypeStruct((batch * seq, 512), BF16),
        scratch_shapes=[pltpu.VMEM((8 + LANE, wide), F32), pltpu.VMEM((1, wide), F32), pltpu.VMEM((1, wide), F32),
                        pltpu.VMEM((NSA_DH, wide), F32)],
        compiler_params=_cparams("parallel", "arbitrary"),
        name="nsa_sel_win_attention",
    )(qn, ks, vs, kw, vw, sel, near_add, win_add, proj, ocmp)


def _t5_bucket_np(dist):
    n = np.maximum(dist, 0)
    max_exact = NUM_BUCKETS // 2
    nf = np.maximum(n, 1).astype(np.float64)
    large = max_exact + (np.log(nf / max_exact) / math.log(MAX_DISTANCE / max_exact)
                         * (NUM_BUCKETS - max_exact)).astype(np.int64)
    large = np.minimum(large, NUM_BUCKETS - 1)
    return np.where(n < max_exact, n, large).astype(np.int32)


def _bias_tables(rel_bias):
    qb = NSA_QBLOCK
    r = np.arange(qb)[:, None]
    l = np.arange(qb)[None, :]
    toep_idx = np.stack([_t5_bucket_np(r - l), _t5_bucket_np(qb + r - l)])
    def lookup(idx):
        onehot = (jnp.asarray(idx.reshape(-1, 1)) == jnp.arange(NUM_BUCKETS)[None, :]).astype(F32)
        return jnp.dot(onehot, rel_bias.astype(F32), precision=HI).reshape(idx.shape + (NSA_HEADS,))

    toep = jnp.transpose(lookup(toep_idx), (3, 0, 1, 2))
    per = qb // NSA_CMP_STRIDE
    band_idx = np.zeros((16, qb, LANE), np.int32)
    for im in range(16):
        base = per * im - 9
        n = base + ((np.arange(LANE) - base) % LANE)
        dist = qb * im + r - NSA_CMP_STRIDE * n[None, :] - (NSA_CMP_LEN - 1)
        band_idx[im] = _t5_bucket_np(dist)
    band = jnp.transpose(lookup(band_idx), (0, 3, 1, 2))
    cfar = rel_bias[NUM_BUCKETS - 1]
    near = jnp.transpose(toep - cfar[:, None, None, None], (1, 3, 0, 2)).reshape(2, qb, NSA_HEADS * qb)
    key = np.arange(qb)[:, None]
    qry = np.tile(np.arange(qb), NSA_HEADS)[None, :]
    diag = jnp.where(key <= qry, near[0], MASKED)
    edge = jnp.asarray(np.where(qry < key, 0.0, MASKED), F32)
    zero = jnp.zeros_like(diag)
    near_add = jnp.concatenate([zero, near[1], diag], axis=0)
    win_add = jnp.concatenate([edge, zero, zero, near[1], diag], axis=0)
    return near_add.astype(F32), win_add.astype(F32), band.astype(F32), cfar.astype(F32)


def _overlap_table(nc, seq):
    n_cmp = (seq - NSA_CMP_LEN) // NSA_CMP_STRIDE + 1
    n_sel = seq // NSA_SEL_BLOCK
    cs = np.arange(nc)[:, None] * NSA_CMP_STRIDE
    ss = np.arange(LANE)[None, :] * NSA_SEL_BLOCK
    ovl = (cs < ss + NSA_SEL_BLOCK) & (cs + NSA_CMP_LEN > ss)
    ovl &= (np.arange(nc)[:, None] < n_cmp) & (np.arange(LANE)[None, :] < n_sel)
    return jnp.asarray(ovl.astype(np.float32), dtype=BF16)


def _pad_axis(t, axis, size):
    pad = [(0, 0)] * t.ndim
    pad[axis] = (0, size - t.shape[axis])
    return jnp.pad(t, pad)


def _pack_w_in(w_in):
    nl, k, _ = w_in.shape
    src = {}
    start = 0
    names = ("gla_q", "gla_k", "gla_v", "gla_g", "gla_lr", "lru_x", "lru_g", "nsa_q", "nsa_kv", "nsa_gate", "rwkv",
             "gates")
    widths = (256, 256, 512, 512, GLA_LOWRANK, 512, 512, 512, 768, 12, 1984, N_BRANCH * D_MODEL)
    for name, wd in zip(names, widths):
        src[name] = (start, wd)
        start += wd
    def piece(name, width):
        s, wd = src[name]
        return _pad_axis(w_in[:, :, s:s + wd], 2, width)
    rs, _ = src["rwkv"]
    rw = jnp.concatenate([
        w_in[:, :, rs:rs + 1536],
        _pad_axis(w_in[:, :, rs + 1536:rs + 1632], 2, LANE),
        _pad_axis(w_in[:, :, rs + 1632:rs + 1728], 2, LANE),
        w_in[:, :, rs + 1728:rs + 1984]], axis=2)
    packed = jnp.concatenate([
        piece("gates", 8192), rw, piece("lru_x", 512), piece("lru_g", 512), piece("nsa_q", 512), piece("gla_v", 512),
        piece("gla_g", 512), piece("gla_q", 256), piece("nsa_kv", 768), piece("gla_k", 256), piece("gla_lr", LANE),
        piece("nsa_gate", LANE)], axis=2)
    assert packed.shape[2] == P_TOTAL
    return packed.astype(BF16)


def _pack_rwkv_vec(t):
    return jnp.concatenate([t[:, :1536], _pad_axis(t[:, 1536:1632], 1, LANE), _pad_axis(t[:, 1632:1728], 1, LANE),
                            t[:, 1728:1984]], axis=1)[:, None, :]


def _row(t):
    return t[:, None, :]


def kernel(x, rel_bias, attn_norm, ffn_norm, w_in, gla_w_gk, gla_b_gk, gla_out_norm, lru_conv_w, lru_conv_b, lru_w_a, lru_b_a, lru_w_i, lru_b_i, lru_lambda, nsa_cmp_pos, nsa_cmp_k1, nsa_cmp_k2, nsa_cmp_v1, nsa_cmp_v2, nsa_q_norm, nsa_k_norm, rwkv_mu, rwkv_w0, rwkv_w_lora, rwkv_a0, rwkv_a_lora, rwkv_g_lora, rwkv_k_k, rwkv_k_a, rwkv_r_k, rwkv_ln_w, rwkv_ln_b, w_branch, w_out, ffn_up, ffn_conv_w, ffn_conv_b, ffn_down):
    batch, seq, d = x.shape
    depth = w_in.shape[0]
    m = batch * seq
    nc = seq // NSA_CMP_STRIDE
    assert d == D_MODEL and seq % 2048 == 0

    w_in_p = _pack_w_in(w_in)
    w_branch_b = _bf(w_branch)
    w_out_b = _bf(w_out)
    ffn_up_b = _bf(ffn_up)
    ffn_down_b = _bf(ffn_down)
    gla_w_gk_p = _pad_axis(gla_w_gk, 1, LANE)
    lru_conv_w_p = _pad_axis(lru_conv_w, 1, 8)
    ffn_conv_w_p = _pad_axis(ffn_conv_w, 1, 8)
    rwkv_w_lora_p = _bf(_pad_axis(rwkv_w_lora, 1, LANE))
    rwkv_a_lora_p = _bf(_pad_axis(rwkv_a_lora, 1, LANE))
    cmp_pos_p = jnp.broadcast_to(nsa_cmp_pos.reshape(depth, 1, NSA_CMP_LEN * NSA_DH), (depth, 8, NSA_CMP_LEN * NSA_DH))
    near_add, win_add, band, cfar = _bias_tables(rel_bias)
    ovl = _overlap_table(nc, seq)

    tm = min(1024, seq)
    xf = x.reshape(m, d)
    for l in range(depth):
        proj = norm_matmul(xf, _row(attn_norm), w_in_p, l, tm, 512)
        y_a = gla_mixer(proj, gla_w_gk_p, _row(gla_b_gk), _row(gla_out_norm), l, batch, seq)
        y_b = lru_mixer(proj, lru_conv_w_p, _row(lru_conv_b), _bf(lru_w_a), _row(lru_b_a), _bf(lru_w_i), _row(lru_b_i),
                        _row(lru_lambda), l, batch, seq, min(256, seq))
        qn, ks, vs, kw, vw = nsa_prep(proj, _row(nsa_q_norm), _row(nsa_k_norm), l, min(512, seq))
        kgrp = proj[:, P_NSA_KV:P_NSA_KV + NSA_DH].reshape(batch, nc, NSA_CMP_STRIDE * NSA_DH)
        vgrp = proj[:, P_NSA_KV + NSA_DH:P_NSA_KV + 2 * NSA_DH].reshape(batch, nc, NSA_CMP_STRIDE * NSA_DH)
        kc, vc = nsa_compress(kgrp, vgrp, cmp_pos_p, _bf(nsa_cmp_k1), _bf(nsa_cmp_k2), _bf(nsa_cmp_v1),
                              _bf(nsa_cmp_v2), _row(nsa_k_norm), l)
        ocmp, sel = nsa_cmp_attention(cfar, qn, kc, vc, band, ovl, proj, batch, seq)
        front = lambda t, tiles: jnp.pad(t.reshape(batch, seq, NSA_DH), ((0, 0), (tiles * NSA_QBLOCK, 0), (0, 0)))
        y_c = nsa_sel_win_attention(qn, front(ks, SEL_PAD_TILES), front(vs, SEL_PAD_TILES), front(kw, WIN_PAD_TILES),
                                    front(vw, WIN_PAD_TILES), sel, near_add, win_add, proj, ocmp, batch, seq)
        y_d = rwkv_mixer(proj, _pack_rwkv_vec(rwkv_mu), _row(rwkv_w0), rwkv_w_lora_p, _row(rwkv_a0), rwkv_a_lora_p,
                         _bf(rwkv_g_lora), _row(rwkv_k_k), _row(rwkv_k_a), _row(rwkv_r_k.reshape(depth, RWKV_WIDTH)),
                         _row(rwkv_ln_w), _row(rwkv_ln_b), l, batch, seq)
        merged = merge_branches((y_a, y_b, y_c, y_d), w_branch_b, proj, l, min(512, seq), 512)
        xf = matmul_residual(merged, w_out_b, xf, l, tm, 512)
        act = ffn_up_conv(xf, _row(ffn_norm), ffn_up_b, ffn_conv_w_p, _row(ffn_conv_b), l, seq, tm, 512)
        xf = matmul_residual(act, ffn_down_b, xf, l, tm, 512)
    return xf.reshape(batch, seq, d)
```

```python
import functools
import math

import numpy as np
import jax
import jax.numpy as jnp
from jax import lax
from jax.experimental import pallas as pl
from jax.experimental.pallas import tpu as pltpu

F32 = jnp.float32
BF16 = jnp.bfloat16
HI = lax.Precision.HIGHEST

LANE = 128
VMEM_LIMIT = 56 * 1024 * 1024

D_MODEL = 2048
N_BRANCH = 4
BRANCH_WIDTH = 512

GLA_HEADS, GLA_DK, GLA_DV, GLA_LOWRANK, GLA_NORMALIZER = 4, 64, 128, 16, 16.0
LRU_WIDTH, LRU_BLOCKS, LRU_CONV, LRU_C = 512, 4, 4, 8.0
NSA_HEADS, NSA_DH = 4, 128
NSA_CMP_LEN, NSA_CMP_STRIDE, NSA_SEL_BLOCK, NSA_SEL_TOPK, NSA_WINDOW, NSA_QBLOCK = 32, 16, 64, 16, 512, 128
RWKV_HEADS, RWKV_DH, RWKV_WIDTH = 8, 64, 512
RWKV_W_LORA, RWKV_A_LORA, RWKV_G_LORA = 96, 96, 256
NUM_BUCKETS, MAX_DISTANCE = 32, 128
D_FF, FFN_CONV = 5632, 3
NEG_BIG = 1e9
MASKED = -1e30
M_INIT = -1e20

P_RWKV, P_LRU_X, P_LRU_G, P_NSA_Q, P_GLA_V, P_GLA_G = 0, 2048, 2560, 3072, 3584, 4096
P_NSA_KV, P_GLA_Q, P_GLA_K, P_GLA_LR, P_NSA_GATE, P_MAIN = 4608, 5376, 5632, 5888, 6016, 6144
P_GATES_TOTAL = 8192
RW_R, RW_K, RW_V, RW_XW, RW_XA, RW_XG, RW_TOTAL = 0, 512, 1024, 1536, 1664, 1792, 2048

CHUNK = 64


def _cparams(*sem):
    return pltpu.CompilerParams(dimension_semantics=sem, vmem_limit_bytes=VMEM_LIMIT)


def _dot(a, b, prec=None):
    return jnp.dot(a, b, preferred_element_type=F32, precision=prec)


def _dot_nt(a, b, prec=None):
    return lax.dot_general(a, b, (((1,), (1,)), ((), ())), preferred_element_type=F32, precision=prec)


def _dot_tn(a, b, prec=None):
    return lax.dot_general(a, b, (((0,), (0,)), ((), ())), preferred_element_type=F32, precision=prec)


def _bf(x):
    return x.astype(BF16)


def _sigmoid(x):
    return 1.0 / (1.0 + jnp.exp(-x))


def _log_sigmoid(x):
    return jnp.minimum(x, 0.0) - jnp.log(1.0 + jnp.exp(-jnp.abs(x)))


def _gelu_tanh(x):
    return 0.5 * x * (1.0 + jnp.tanh(math.sqrt(2.0 / math.pi) * (x + 0.044715 * (x * x * x))))


def _iota(shape, dim):
    return lax.broadcasted_iota(jnp.int32, shape, dim)


def _idiv(x, d):
    assert d & (d - 1) == 0
    return jnp.right_shift(x, d.bit_length() - 1)


def _shift_rows(x, s, top):
    xr = pltpu.roll(x, s, 0)
    tr = pltpu.roll(top, s, 0)
    row = _iota((8, x.shape[1]), 0)
    head = jnp.where(row < s, tr, xr[:8])
    return jnp.concatenate([head, xr[8:]], axis=0)


def _norm_mm_kernel(x_ref, g_ref, w_ref, og_ref, om_ref, xn_ref, *, gate_tiles):
    j = pl.program_id(1)

    @pl.when(j == 0)
    def _():
        x = x_ref[...]
        ms = jnp.mean(x * x, axis=-1, keepdims=True)
        xn_ref[...] = _bf(x * lax.rsqrt(ms + 1e-6) * g_ref[...])

    acc = _dot(xn_ref[...], w_ref[...])

    @pl.when(j < gate_tiles)
    def _():
        og_ref[...] = _bf(acc)

    @pl.when(j >= gate_tiles)
    def _():
        om_ref[...] = acc


def norm_matmul(x, gain, w, layer, tm, tn):
    m, k = x.shape
    n = w.shape[2]
    gate_tiles = P_GATES_TOTAL // tn
    kern = functools.partial(_norm_mm_kernel, gate_tiles=gate_tiles)
    return pl.pallas_call(
        kern,
        grid=(m // tm, n // tn),
        in_specs=[pl.BlockSpec((tm, k), lambda i, j: (i, 0)),
                  pl.BlockSpec((None, 1, k), lambda i, j: (layer, 0, 0)),
                  pl.BlockSpec((None, k, tn), lambda i, j: (layer, 0, j))],
        out_specs=[pl.BlockSpec((tm, tn), lambda i, j: (i, jnp.minimum(j, gate_tiles - 1))),
                   pl.BlockSpec((tm, tn), lambda i, j: (i, jnp.maximum(j - gate_tiles, 0)))],
        out_shape=[jax.ShapeDtypeStruct((m, P_GATES_TOTAL), BF16), jax.ShapeDtypeStruct((m, n - P_GATES_TOTAL), F32)],
        scratch_shapes=[pltpu.VMEM((tm, k), BF16)],
        compiler_params=_cparams("parallel", "arbitrary"),
        name="norm_matmul",
    )(x, gain, w)


def _mm_res_kernel(a_ref, w_ref, r_ref, o_ref):
    o_ref[...] = r_ref[...] + _dot(a_ref[...], w_ref[...])


def matmul_residual(a, w, res, layer, tm, tn):
    m, k = a.shape
    n = w.shape[2]
    return pl.pallas_call(
        _mm_res_kernel,
        grid=(m // tm, n // tn),
        in_specs=[pl.BlockSpec((tm, k), lambda i, j: (i, 0)),
                  pl.BlockSpec((None, k, tn), lambda i, j: (layer, 0, j)),
                  pl.BlockSpec((tm, tn), lambda i, j: (i, j))],
        out_specs=pl.BlockSpec((tm, tn), lambda i, j: (i, j)),
        out_shape=jax.ShapeDtypeStruct((m, n), F32),
        compiler_params=_cparams("parallel", "arbitrary"),
        name="matmul_residual",
    )(a, w, res)


def _merge_kernel(ya_ref, yb_ref, yc_ref, yd_ref, wb_ref, g0_ref, g1_ref, g2_ref, g3_ref, o_ref):
    acc = None
    for n, (y_ref, g_ref) in enumerate(((ya_ref, g0_ref), (yb_ref, g1_ref), (yc_ref, g2_ref), (yd_ref, g3_ref))):
        t = _sigmoid(g_ref[...].astype(F32)) * _dot(y_ref[...], wb_ref[n])
        acc = t if acc is None else acc + t
    o_ref[...] = _bf(acc)


def merge_branches(ys, w_branch, gates, layer, tm, tn):
    m = ys[0].shape[0]
    nj = D_MODEL // tn
    y_spec = pl.BlockSpec((tm, BRANCH_WIDTH), lambda i, j: (i, 0))
    gate_specs = [pl.BlockSpec((tm, tn), functools.partial(lambda i, j, n: (i, (n * D_MODEL) // tn + j), n=n))
                  for n in range(N_BRANCH)]
    return pl.pallas_call(
        _merge_kernel,
        grid=(m // tm, nj),
        in_specs=[y_spec] * 4 + [pl.BlockSpec((None, N_BRANCH, BRANCH_WIDTH, tn), lambda i, j: (layer, 0, 0, j))]
        + gate_specs,
        out_specs=pl.BlockSpec((tm, tn), lambda i, j: (i, j)),
        out_shape=jax.ShapeDtypeStruct((m, D_MODEL), BF16),
        compiler_params=_cparams("parallel", "arbitrary"),
        name="merge_branches",
    )(*ys, w_branch, gates, gates, gates, gates)


def _ffn_up_kernel(x_ref, g_ref, wg_ref, wv_ref, cw_g_ref, cw_v_ref, cb_g_ref, cb_v_ref, o_ref,
                   xn_ref, carry_ref, *, tiles_per_seq):
    i, j = pl.program_id(0), pl.program_id(1)

    @pl.when(j == 0)
    def _():
        x = x_ref[...]
        ms = jnp.mean(x * x, axis=-1, keepdims=True)
        xn_ref[...] = _bf(x * lax.rsqrt(ms + 1e-6) * g_ref[...])

    first = (i % tiles_per_seq) == 0

    @pl.when(first)
    def _():
        carry_ref[j] = jnp.zeros(carry_ref.shape[1:], F32)

    xn = xn_ref[...]
    outs = []
    for half, (w_ref, cw_ref, cb_ref) in enumerate(((wg_ref, cw_g_ref, cb_g_ref), (wv_ref, cw_v_ref, cb_v_ref))):
        u = _dot(xn, w_ref[...])
        top = carry_ref[j, half]
        carry_ref[j, half] = u[u.shape[0] - 8:]
        cw = cw_ref[...]
        outs.append(cw[0:1] * _shift_rows(u, 2, top) + cw[1:2] * _shift_rows(u, 1, top) + cw[2:3] * u + cb_ref[...])
    gate, val = outs
    o_ref[...] = _bf(gate * _sigmoid(gate) * val)


def ffn_up_conv(x, gain, w_up, conv_w, conv_b, layer, seq, tm, tn):
    m, k = x.shape
    nj = D_FF // tn
    kern = functools.partial(_ffn_up_kernel, tiles_per_seq=seq // tm)
    return pl.pallas_call(
        kern,
        grid=(m // tm, nj),
        in_specs=[pl.BlockSpec((tm, k), lambda i, j: (i, 0)),
                  pl.BlockSpec((None, 1, k), lambda i, j: (layer, 0, 0)),
                  pl.BlockSpec((None, k, tn), lambda i, j: (layer, 0, j)),
                  pl.BlockSpec((None, k, tn), lambda i, j: (layer, 0, nj + j)),
                  pl.BlockSpec((None, 8, tn), lambda i, j: (layer, 0, j)),
                  pl.BlockSpec((None, 8, tn), lambda i, j: (layer, 0, nj + j)),
                  pl.BlockSpec((None, 1, tn), lambda i, j: (layer, 0, j)),
                  pl.BlockSpec((None, 1, tn), lambda i, j: (layer, 0, nj + j))],
        out_specs=pl.BlockSpec((tm, tn), lambda i, j: (i, j)),
        out_shape=jax.ShapeDtypeStruct((m, D_FF), BF16),
        scratch_shapes=[pltpu.VMEM((tm, k), BF16), pltpu.VMEM((nj, 2, 8, tn), F32)],
        compiler_params=_cparams("arbitrary", "arbitrary"),
        name="ffn_up",
    )(x, gain, w_up, w_up, conv_w, conv_w, conv_b, conv_b)


def _gla_kernel(q_ref, k_ref, v_ref, g_ref, lr_ref, wgk_ref, bgk_ref, gain_ref, o_ref, st_ref):
    c = CHUNK

    nb = q_ref.shape[0]
    npair = GLA_HEADS // 2

    @pl.when(pl.program_id(0) == 0)
    def _():
        st_ref[...] = jnp.zeros_like(st_ref)

    row = _iota((c, c), 0)
    col = _iota((c, c), 1)
    tri_incl = (col <= row).astype(F32)
    lane = _iota((1, LANE), 1)
    bd = _idiv(_iota((2 * GLA_DV, LANE), 0), GLA_DV) == _idiv(_iota((2 * GLA_DV, LANE), 1), GLA_DK)
    sub = 16
    nsub = c // sub
    scores = {}
    o_inter = {}
    vbs = []
    for bi in range(nb):
        log_a = _log_sigmoid(_dot(lr_ref[bi], wgk_ref[...], HI) + bgk_ref[...]) * (1.0 / GLA_NORMALIZER)
        b = _dot(tri_incl, log_a, HI)
        vb = _bf(v_ref[bi])
        vbs.append(vb)
        for p in range(npair):
            sl = slice(p * LANE, (p + 1) * LANE)
            qp = q_ref[bi, :, sl] * (GLA_DK ** -0.5)
            kp = k_ref[bi, :, sl]
            bp = b[:, sl]
            b_last = bp[c - 1:c]
            st = st_ref[bi * npair + p]
            o_inter[bi, p] = _dot_nt(_bf(qp * jnp.exp(bp)), _bf(st))
            upd = _dot_tn(vb[:, 2 * p * GLA_DV:(2 * p + 2) * GLA_DV], _bf(kp * jnp.exp(b_last - bp)))
            st_ref[bi * npair + p] = st * jnp.exp(b_last) + jnp.where(bd, upd, 0.0)
            for i in range(nsub):
                rows = slice(i * sub, (i + 1) * sub)
                nrow = (i + 1) * sub
                bref = bp[i * sub - 1:i * sub] if i > 0 else jnp.zeros((1, LANE), F32)
                qi = qp[rows] * jnp.exp(bp[rows] - bref)
                ki = _bf(kp[:nrow] * jnp.exp(jnp.minimum(bref - bp[:nrow], 80.0)))
                causal = _iota((sub, nrow), 1) <= (_iota((sub, nrow), 0) + i * sub)
                for hh in range(2):
                    head_lanes = _idiv(lane, GLA_DK) == hh
                    s = _dot_nt(_bf(jnp.where(head_lanes, qi, 0.0)), ki)
                    scores[bi, 2 * p + hh, i] = _bf(jnp.where(causal, s, 0.0))
    gain = gain_ref[...]
    for bi in range(nb):
        for h in range(GLA_HEADS):
            hs = slice(h * GLA_DV, (h + 1) * GLA_DV)
            intra = [_dot(scores[bi, h, i], vbs[bi][:(i + 1) * sub, hs]) for i in range(nsub)]
            hh = h % 2
            o = o_inter[bi, h // 2][:, hh * GLA_DV:(hh + 1) * GLA_DV] + jnp.concatenate(intra, axis=0)
            y = o * lax.rsqrt(jnp.mean(o * o, axis=-1, keepdims=True) + 1e-6) * gain
            g = g_ref[bi, :, hs]
            o_ref[bi, :, hs] = _bf(y * (g * _sigmoid(g)))


def gla_mixer(proj, w_gk, b_gk, out_gain, layer, batch, seq):
    nt = seq // CHUNK
    def col(off, width):
        return pl.BlockSpec((batch, CHUNK, width), lambda t: (0, t, off // width))
    proj3 = proj.reshape(batch, seq, proj.shape[1])
    out = pl.pallas_call(
        _gla_kernel,
        grid=(nt,),
        in_specs=[col(P_GLA_Q, 256), col(P_GLA_K, 256), col(P_GLA_V, 512), col(P_GLA_G, 512), col(P_GLA_LR, 128),
                  pl.BlockSpec((None, LANE, 256), lambda t: (layer, 0, 0)),
                  pl.BlockSpec((None, 1, 256), lambda t: (layer, 0, 0)),
                  pl.BlockSpec((None, 1, GLA_DV), lambda t: (layer, 0, 0))],
        out_specs=pl.BlockSpec((batch, CHUNK, 512), lambda t: (0, t, 0)),
        out_shape=jax.ShapeDtypeStruct((batch, seq, 512), BF16),
        scratch_shapes=[pltpu.VMEM((batch * (GLA_HEADS // 2), 2 * GLA_DV, LANE), F32)],
        compiler_params=_cparams("arbitrary"),
        name="gla_mixer",
    )(proj3, proj3, proj3, proj3, proj3, w_gk, b_gk, out_gain)
    return out.reshape(batch * seq, 512)


def _lru_kernel(x_ref, gb_ref, cw_ref, cb_ref, wa_ref, ba_ref, wi_ref, bi_ref, lam_ref, o_ref, xc_ref, h_ref):
    t = x_ref.shape[0]

    @pl.when(pl.program_id(1) == 0)
    def _():
        xc_ref[...] = jnp.zeros_like(xc_ref)
        h_ref[...] = jnp.zeros_like(h_ref)

    x = x_ref[...]
    top = xc_ref[...]
    xc_ref[...] = x[t - 8:]
    cw = cw_ref[...]
    xc = (cw[0:1] * _shift_rows(x, 3, top) + cw[1:2] * _shift_rows(x, 2, top) + cw[2:3] * _shift_rows(x, 1, top)
          + cw[3:4] * x + cb_ref[...])
    xcb = _bf(xc)
    ra, ri = [], []
    for n in range(LRU_BLOCKS):
        blk = xcb[:, n * LANE:(n + 1) * LANE]
        ra.append(_dot(blk, wa_ref[n]))
        ri.append(_dot(blk, wi_ref[n]))
    r = _sigmoid(jnp.concatenate(ra, axis=1) + ba_ref[...])
    gi = _sigmoid(jnp.concatenate(ri, axis=1) + bi_ref[...])
    log_a = LRU_C * r * _log_sigmoid(lam_ref[...])
    a = jnp.exp(log_a)
    u = jnp.sqrt(-jnp.tanh(log_a) * (a * a + 1.0)) * (gi * xc)
    row = _iota((t, LRU_WIDTH), 0)
    k = 1
    while k < t:
        a_sh = jnp.where(row < k, 1.0, pltpu.roll(a, k, 0))
        u_sh = jnp.where(row < k, 0.0, pltpu.roll(u, k, 0))
        u = u + a * u_sh
        a = a * a_sh
        k *= 2
    h = a * h_ref[0:1] + u
    h_ref[...] = jnp.broadcast_to(h[t - 1:t], h_ref.shape)
    o_ref[...] = _bf(h * _gelu_tanh(gb_ref[...]))


def lru_mixer(proj, conv_w, conv_b, w_a, b_a, w_i, b_i, lam, layer, batch, seq, tt):
    nt = seq // tt
    vec = pl.BlockSpec((None, 1, LRU_WIDTH), lambda b, t: (layer, 0, 0))
    wblk = pl.BlockSpec((None, LRU_BLOCKS, LANE, LANE), lambda b, t: (layer, 0, 0, 0))
    return pl.pallas_call(
        _lru_kernel,
        grid=(batch, nt),
        in_specs=[pl.BlockSpec((tt, 512), lambda b, t: (b * nt + t, P_LRU_X // 512)),
                  pl.BlockSpec((tt, 512), lambda b, t: (b * nt + t, P_LRU_G // 512)),
                  pl.BlockSpec((None, 8, LRU_WIDTH), lambda b, t: (layer, 0, 0)), vec, wblk, vec, wblk, vec, vec],
        out_specs=pl.BlockSpec((tt, 512), lambda b, t: (b * nt + t, 0)),
        out_shape=jax.ShapeDtypeStruct((batch * seq, 512), BF16),
        scratch_shapes=[pltpu.VMEM((8, LRU_WIDTH), F32), pltpu.VMEM((8, LRU_WIDTH), F32)],
        compiler_params=_cparams("parallel", "arbitrary"),
        name="lru_mixer",
    )(proj, proj, conv_w, conv_b, w_a, b_a, w_i, b_i, lam)


def _rwkv_kernel(f_ref, mu_ref, w0_ref, wl_ref, a0_ref, al_ref, gl_ref, kk_ref, ka_ref, rk_ref, lnw_ref, lnb_ref,
                 o_ref, st_ref, prev_ref):
    c = CHUNK
    n = RWKV_DH

    nb = f_ref.shape[0]

    @pl.when(pl.program_id(0) == 0)
    def _():
        st_ref[...] = jnp.zeros_like(st_ref)
        prev_ref[...] = jnp.zeros_like(prev_ref)

    feats = [f_ref[b] for b in range(nb)]
    prev = jnp.concatenate([_shift_rows(feats[b], 1, prev_ref[b]) for b in range(nb)], axis=0)
    for b in range(nb):
        prev_ref[b] = feats[b][c - 8:]
    feat = jnp.concatenate(feats, axis=0)
    xm = feat + (prev - feat) * mu_ref[...]
    r = xm[:, RW_R:RW_R + 512]
    k = xm[:, RW_K:RW_K + 512]
    v = xm[:, RW_V:RW_V + 512]
    log_w = -math.exp(-0.5) * _sigmoid(w0_ref[...] + _dot(_bf(jnp.tanh(xm[:, RW_XW:RW_XW + LANE])), wl_ref[...]))
    a = _sigmoid(a0_ref[...] + _dot(_bf(xm[:, RW_XA:RW_XA + LANE]), al_ref[...]))
    g = _dot(_bf(_sigmoid(xm[:, RW_XG:RW_XG + 256])), gl_ref[...])

    assert c == n and 2 * n == LANE
    head0 = _iota((1, LANE), 1) < n
    same_head = _idiv(_iota((LANE, LANE), 0), n) == _idiv(_iota((LANE, LANE), 1), n)
    head_ones = _bf(same_head.astype(F32))
    row = _iota((c, LANE), 0)
    col = jnp.bitwise_and(_iota((c, LANE), 1), n - 1)
    tri_incl = col <= row
    tri_strict = col < row
    rr_ = _iota((nb * c, nb * c), 0)
    cc_ = _iota((nb * c, nb * c), 1)
    tri_sq = _bf(((cc_ <= rr_) & (_idiv(cc_, c) == _idiv(rr_, c))).astype(F32))

    def stack(t):
        return jnp.concatenate([jnp.where(head0, t, 0.0), jnp.where(head0, 0.0, t)], axis=0)

    def seg_sum(t):
        hi = _bf(t)
        return _dot(hi, head_ones) + _dot(_bf(t - hi.astype(F32)), head_ones)

    lw_hi = _bf(log_w)
    lw_r = log_w - lw_hi.astype(F32)
    lw_mid = _bf(lw_r)
    cum_all = _dot(tri_sq, lw_hi) + _dot(tri_sq, lw_mid) + _dot(tri_sq, _bf(lw_r - lw_mid.astype(F32)))

    pairs = range(RWKV_HEADS // 2)
    sls = [slice(p * LANE, (p + 1) * LANE) for p in pairs]
    kk = [k[:, sl] * kk_ref[:, sl] for sl in sls]
    ss = [seg_sum(t * t) for t in kk]
    kk = [t / jnp.maximum(jnp.sqrt(s), 1e-12) for t, s in zip(kk, ss)]
    k2 = [k[:, sl] * (1.0 + (a[:, sl] - 1.0) * ka_ref[:, sl]) for sl in sls]
    e_neg = [jnp.exp(-cum_all[:, sl]) for sl in sls]
    rt_all = [_bf(r[:, sl] * jnp.exp(cum_all[:, sl])) for sl in sls]
    kt_all = [_bf(kk[p] * jnp.exp(cum_all[:, sls[p]] - log_w[:, sls[p]])) for p in pairs]
    kb_all = [k2[p] * e_neg[p] for p in pairs]
    bb_all = [kk[p] * a[:, sls[p]] * e_neg[p] for p in pairs]
    chains = [(b, p) for b in range(nb) for p in pairs]
    nch = range(len(chains))
    rows = [slice(b * c, (b + 1) * c) for b, _ in chains]
    e_last = [jnp.exp(cum_all[b * c + c - 1:(b + 1) * c, sls[p]]) for b, p in chains]
    rt = [rt_all[p][rows[i]] for i, (_, p) in enumerate(chains)]
    kt = [kt_all[p][rows[i]] for i, (_, p) in enumerate(chains)]
    kb = [kb_all[p][rows[i]] for i, (_, p) in enumerate(chains)]
    bb = [bb_all[p][rows[i]] for i, (_, p) in enumerate(chains)]
    vv = [v[rows[i], sls[p]] for i, (_, p) in enumerate(chains)]
    kb2 = [_bf(stack(t)) for t in kb]
    bb2 = [_bf(stack(t)) for t in bb]
    vstk = [_bf(stack(t)) for t in vv]
    st = [st_ref[i] for i in nch]
    stb = [_bf(t) for t in st]
    a_mat = [jnp.where(tri_strict, _dot_nt(kt[i], bb2[i]), 0.0) for i in nch]
    b_mat = [jnp.where(tri_strict, _dot_nt(kt[i], kb2[i]), 0.0) for i in nch]
    rr_mat = [jnp.where(tri_incl, _dot_nt(rt[i], kb2[i]), 0.0) for i in nch]
    rb_mat = [jnp.where(tri_incl, _dot_nt(rt[i], bb2[i]), 0.0) for i in nch]
    rhs = [_dot_nt(kt[i], stb[i]) + _dot(_bf(b_mat[i]), vstk[i]) for i in nch]
    y0 = [_dot_nt(rt[i], stb[i]) + _dot(_bf(rr_mat[i]), vstk[i]) for i in nch]
    nmat = [-t for t in a_mat]
    tm1 = list(nmat)
    pw = 1
    while 2 * pw < c:
        nmat = [_dot(_bf(t), _bf(stack(t))) for t in nmat]
        tm1 = [tm1[i] + nmat[i] + _dot(_bf(tm1[i]), _bf(stack(nmat[i]))) for i in nch]
        pw *= 2
    u = [rhs[i] + _dot(_bf(tm1[i]), _bf(stack(rhs[i]))) for i in nch]
    y = [y0[i] - _dot(_bf(rb_mat[i]), _bf(stack(u[i]))) for i in nch]
    upd = [_dot_tn(_bf(jnp.concatenate([vv[i], -u[i]], axis=0)),
                   _bf(jnp.concatenate([kb[i], bb[i]], axis=0) * e_last[i])) for i in nch]
    for i in nch:
        st_ref[i] = st[i] * e_last[i] + jnp.where(same_head, upd[i], 0.0)
    y_all = [jnp.concatenate([y[b * len(pairs) + p] for b in range(nb)], axis=0) for p in pairs]
    yc = [y_all[p] - seg_sum(y_all[p]) * (1.0 / n) for p in pairs]
    var = [seg_sum(t * t) * (1.0 / n) for t in yc]
    bonus = [seg_sum(r[:, sls[p]] * k2[p] * rk_ref[:, sls[p]]) for p in pairs]
    out = [(yc[p] * lax.rsqrt(var[p] + 64e-5) * lnw_ref[:, sls[p]] + lnb_ref[:, sls[p]] + bonus[p] * v[:, sls[p]])
           for p in pairs]
    out = _bf(jnp.concatenate(out, axis=1) * g)
    for b in range(nb):
        o_ref[b] = out[b * c:(b + 1) * c]


def rwkv_mixer(proj, mu, w0, w_lora, a0, a_lora, g_lora, k_k, k_a, r_k, ln_w, ln_b, layer, batch, seq):
    nt = seq // CHUNK
    vec = pl.BlockSpec((None, 1, RWKV_WIDTH), lambda t: (layer, 0, 0))
    out = pl.pallas_call(
        _rwkv_kernel,
        grid=(nt,),
        in_specs=[pl.BlockSpec((batch, CHUNK, RW_TOTAL), lambda t: (0, t, P_RWKV // RW_TOTAL)),
                  pl.BlockSpec((None, 1, RW_TOTAL), lambda t: (layer, 0, 0)),
                  vec, pl.BlockSpec((None, LANE, RWKV_WIDTH), lambda t: (layer, 0, 0)),
                  vec, pl.BlockSpec((None, LANE, RWKV_WIDTH), lambda t: (layer, 0, 0)),
                  pl.BlockSpec((None, RWKV_G_LORA, RWKV_WIDTH), lambda t: (layer, 0, 0)),
                  vec, vec, vec, vec, vec],
        out_specs=pl.BlockSpec((batch, CHUNK, 512), lambda t: (0, t, 0)),
        out_shape=jax.ShapeDtypeStruct((batch, seq, 512), BF16),
        scratch_shapes=[pltpu.VMEM((batch * (RWKV_HEADS // 2), LANE, LANE), F32),
                        pltpu.VMEM((batch, 8, RW_TOTAL), F32)],
        compiler_params=_cparams("arbitrary"),
        name="rwkv_mixer",
    )(proj.reshape(batch, seq, proj.shape[1]), mu, w0, w_lora, a0, a_lora, g_lora, k_k, k_a, r_k, ln_w, ln_b)
    return out.reshape(batch * seq, 512)


def _head_rms(x, gain):
    return x * lax.rsqrt(jnp.mean(x * x, axis=-1, keepdims=True) + 1e-6) * gain


def _nsa_prep_kernel(q_ref, kv_ref, qg_ref, kg_ref, qn_ref, ks_ref, vs_ref, kw_ref, vw_ref):
    qg = qg_ref[...] * (NSA_DH ** -0.5)
    for h in range(NSA_HEADS):
        sl = slice(h * NSA_DH, (h + 1) * NSA_DH)
        qn_ref[:, sl] = _bf(_head_rms(q_ref[:, sl], qg))
    kg = kg_ref[...]
    ks_ref[...] = _bf(_head_rms(kv_ref[:, 2 * NSA_DH:3 * NSA_DH], kg))
    vs_ref[...] = _bf(kv_ref[:, 3 * NSA_DH:4 * NSA_DH])
    kw_ref[...] = _bf(_head_rms(kv_ref[:, 4 * NSA_DH:5 * NSA_DH], kg))
    vw_ref[...] = _bf(kv_ref[:, 5 * NSA_DH:6 * NSA_DH])


def nsa_prep(proj, q_gain, k_gain, layer, tt):
    m = proj.shape[0]
    gain = pl.BlockSpec((None, 1, NSA_DH), lambda i: (layer, 0, 0))
    kv_out = pl.BlockSpec((tt, NSA_DH), lambda i: (i, 0))
    kv_shape = jax.ShapeDtypeStruct((m, NSA_DH), BF16)
    return pl.pallas_call(
        _nsa_prep_kernel,
        grid=(m // tt,),
        in_specs=[pl.BlockSpec((tt, 512), lambda i: (i, P_NSA_Q // 512)),
                  pl.BlockSpec((tt, 768), lambda i: (i, P_NSA_KV // 768)), gain, gain],
        out_specs=[pl.BlockSpec((tt, 512), lambda i: (i, 0)), kv_out, kv_out, kv_out, kv_out],
        out_shape=[jax.ShapeDtypeStruct((m, 512), BF16), kv_shape, kv_shape, kv_shape, kv_shape],
        compiler_params=_cparams("parallel"),
        name="nsa_prep",
    )(proj, proj, q_gain, k_gain)


def _nsa_compress_kernel(kg_ref, vg_ref, pos_ref, k1_ref, k2_ref, v1_ref, v2_ref, gain_ref, kc_ref, vc_ref):
    nc = kg_ref.shape[0]
    half = NSA_CMP_STRIDE * NSA_DH
    pos = _bf(pos_ref[...])

    def compress(g_ref, w1_ref, w2_ref):
        grp = _bf(g_ref[...])
        first = _dot(grp, w1_ref[:half])
        second = _dot(grp, w1_ref[half:])
        const = _dot(pos, w1_ref[...])[0:1]
        hid = first + pltpu.roll(second, nc - 1, 0) + const
        return _dot(_bf(_gelu_tanh(hid)), w2_ref[...])

    kc_ref[...] = _bf(_head_rms(compress(kg_ref, k1_ref, k2_ref), gain_ref[...]))
    vc_ref[...] = _bf(compress(vg_ref, v1_ref, v2_ref))


def nsa_compress(kgrp, vgrp, pos, k1, k2, v1, v2, k_gain, layer):
    batch, nc, width = kgrp.shape
    grp = pl.BlockSpec((None, nc, width), lambda b: (b, 0, 0))
    w1 = pl.BlockSpec((None, 2 * width, NSA_DH), lambda b: (layer, 0, 0))
    w2 = pl.BlockSpec((None, NSA_DH, NSA_DH), lambda b: (layer, 0, 0))
    out = pl.BlockSpec((None, nc, NSA_DH), lambda b: (b, 0, 0))
    shape = jax.ShapeDtypeStruct((batch, nc, NSA_DH), BF16)
    return pl.pallas_call(
        _nsa_compress_kernel,
        grid=(batch,),
        in_specs=[grp, grp, pl.BlockSpec((None, 8, 2 * width), lambda b: (layer, 0, 0)), w1, w2, w1, w2,
                  pl.BlockSpec((None, 1, NSA_DH), lambda b: (layer, 0, 0))],
        out_specs=[out, out],
        out_shape=[shape, shape],
        compiler_params=_cparams("parallel"),
        name="nsa_compress",
    )(kgrp, vgrp, pos, k1, k2, v1, v2, k_gain)


CMP_GROUP = 4


def _nsa_cmp_kernel(cfar_ref, q_ref, kc_ref, vc_ref, *rest):
    band_refs = rest[:CMP_GROUP]
    ovl_ref, gate_ref, o_ref, sel_ref = rest[CMP_GROUP:]
    qb = NSA_QBLOCK
    nc = kc_ref.shape[0]
    kc = kc_ref[...]
    vc = vc_ref[...]
    ovl = ovl_ref[...]
    r = _iota((qb, nc), 0)
    ncol = _iota((qb, nc), 1)
    blk = _iota((qb, LANE), 1)
    works = []
    for g in range(CMP_GROUP):
        i = pl.program_id(1) * CMP_GROUP + g
        rows = slice(g * qb, (g + 1) * qb)
        dist = qb * i + r - NSA_CMP_STRIDE * ncol - (NSA_CMP_LEN - 1)
        visible = dist >= 0
        lo = (qb // NSA_CMP_STRIDE) * i - 9
        in_band = (ncol >= lo) & (ncol <= lo + 15)
        gate = gate_ref[rows, :]
        p_sum = jnp.zeros((qb, nc), F32)
        for h in range(NSA_HEADS):
            sl = slice(h * NSA_DH, (h + 1) * NSA_DH)
            band = jnp.concatenate([band_refs[g][h]] * (nc // LANE), axis=1)
            logit = _dot_nt(q_ref[rows, sl], kc) + jnp.where(in_band, band, cfar_ref[h])
            logit = jnp.where(visible, logit, MASKED)
            mx = jnp.max(logit, axis=-1, keepdims=True)
            p = jnp.where(visible, jnp.exp(logit - mx), 0.0)
            p = p / jnp.maximum(jnp.sum(p, axis=-1, keepdims=True), 1e-30)
            p_sum = p_sum + p
            o_ref[rows, sl] = _sigmoid(gate[:, h:h + 1]) * _dot(_bf(p), vc)
        p_hi = _bf(p_sum)
        score = _dot(p_hi, ovl) + _dot(_bf(p_sum - p_hi.astype(F32)), ovl)
        pos = qb * i + _iota((qb, LANE), 0)
        cur = _idiv(pos, NSA_SEL_BLOCK)
        forced = (blk == 0) | (blk == cur) | (blk == cur - 1)
        works.append(jnp.where(forced, NEG_BIG, jnp.where(blk * NSA_SEL_BLOCK <= pos, score, -NEG_BIG)))
    work = jnp.concatenate(works, axis=0)
    sel = jnp.zeros(work.shape, F32)
    blk_f = _iota(work.shape, 1).astype(F32)
    for _ in range(NSA_SEL_TOPK):
        mx = jnp.max(work, axis=-1, keepdims=True)
        first = jnp.min(jnp.where(work == mx, blk_f, float(LANE)), axis=-1, keepdims=True)
        pick = blk_f == first
        sel = jnp.where(pick, 1.0, sel)
        work = jnp.where(pick, -jnp.inf, work)
    sel_ref[...] = _bf(sel)


def nsa_cmp_attention(cfar, qn, kc, vc, band, ovl, proj, batch, seq):
    rows = CMP_GROUP * NSA_QBLOCK
    ns = seq // rows
    nc = kc.shape[1]
    full = pl.BlockSpec((None, nc, NSA_DH), lambda b, i: (b, 0, 0))
    band_specs = [pl.BlockSpec((None, NSA_HEADS, NSA_QBLOCK, LANE),
                               functools.partial(lambda b, i, g: ((i * CMP_GROUP + g) % 16, 0, 0, 0), g=g))
                  for g in range(CMP_GROUP)]
    return pl.pallas_call(
        _nsa_cmp_kernel,
        grid=(batch, ns),
        in_specs=[pl.BlockSpec(memory_space=pltpu.SMEM),
                  pl.BlockSpec((rows, 512), lambda b, i: (b * ns + i, 0)), full, full] + band_specs
        + [pl.BlockSpec((nc, LANE), lambda b, i: (0, 0)),
           pl.BlockSpec((rows, LANE), lambda b, i: (b * ns + i, P_NSA_GATE // LANE))],
        out_specs=[pl.BlockSpec((rows, 512), lambda b, i: (b * ns + i, 0)),
                   pl.BlockSpec((rows, LANE), lambda b, i: (b * ns + i, 0))],
        out_shape=[jax.ShapeDtypeStruct((batch * seq, 512), F32), jax.ShapeDtypeStruct((batch * seq, LANE), BF16)],
        compiler_params=_cparams("parallel", "arbitrary"),
        name="nsa_cmp_attention",
    )(cfar, qn, kc, vc, *([band] * CMP_GROUP), ovl, proj)


FAR_GROUP = 4
SEL_PAD_TILES = FAR_GROUP
WIN_PAD_TILES = NSA_WINDOW // NSA_QBLOCK


def _nsa_sel_win_kernel(q_ref, ks_ref, vs_ref, kw_ref, vw_ref, sel_ref, near_ref, win_ref, gate_ref, ocmp_ref,
                        o_ref, selt_ref, m_ref, l_ref, acc_ref):
    qb = NSA_QBLOCK
    nh = NSA_HEADS
    i = pl.program_id(1)
    qt = jnp.concatenate([_bf(q_ref[:, h * NSA_DH:(h + 1) * NSA_DH].astype(F32).T) for h in range(nh)], axis=1)
    selt = sel_ref[...].astype(F32).T
    selt_ref[0:8, :] = jnp.full((8, nh * qb), MASKED, F32)
    selt_ref[8:, :] = jnp.concatenate([jnp.where(selt > 0.5, 0.0, MASKED)] * nh, axis=1)
    pad_rows = 8 - 2 * SEL_PAD_TILES

    def reset():
        m_ref[...] = jnp.full_like(m_ref, M_INIT)
        l_ref[...] = jnp.zeros_like(l_ref)
        acc_ref[...] = jnp.zeros_like(acc_ref)

    def chosen(pt, ntile):
        first = pad_rows + 2 * pt
        rows = [jnp.broadcast_to(selt_ref[pl.ds(first + j, 1), :], (NSA_SEL_BLOCK, nh * qb)) for j in range(2 * ntile)]
        return jnp.concatenate(rows, axis=0)

    def tile(ref, pt, ntile=1):
        return ref[pl.ds(pl.multiple_of(pt * qb, qb), ntile * qb), :]

    reset()
    near = i + SEL_PAD_TILES - 2
    ngroup = (jnp.maximum(i - 2, 0) + FAR_GROUP - 1) // FAR_GROUP
    ones_rows = jnp.ones((16, (FAR_GROUP // 2) * qb), BF16)

    def far_body(g, carry):
        half = FAR_GROUP // 2
        pa = near - FAR_GROUP * (g + 1)
        pb = pa + half
        sa = _bf(_dot(tile(ks_ref, pa, half), qt) + chosen(pa, half))
        sb = _bf(_dot(tile(ks_ref, pb, half), qt) + chosen(pb, half))
        m0 = m_ref[...]
        ma = jnp.maximum(m0, jnp.max(sa, axis=0, keepdims=True).astype(F32))
        p_a = jnp.exp(sa - _bf(ma))
        pv_a = _dot_tn(tile(vs_ref, pa, half), p_a)
        mb = jnp.maximum(ma, jnp.max(sb, axis=0, keepdims=True).astype(F32))
        p_b = jnp.exp(sb - _bf(mb))
        pv_b = _dot_tn(tile(vs_ref, pb, half), p_b)
        al_a = jnp.exp(m0 - ma)
        al_b = jnp.exp(ma - mb)
        sum_a = jnp.sum(p_a.astype(F32), axis=0, keepdims=True)
        sum_b = jnp.sum(p_b.astype(F32), axis=0, keepdims=True)
        l_ref[...] = al_b * (al_a * l_ref[...] + sum_a) + sum_b
        acc_ref[...] = al_b * (al_a * acc_ref[...] + pv_a) + pv_b
        m_ref[...] = mb
        return carry

    lax.fori_loop(0, ngroup, far_body, 0)

    nw = NSA_WINDOW // qb
    adds = []
    for d in range(nw + 1):
        add = win_ref[d * qb:(d + 1) * qb, :]
        if d < nw:
            add = add + jnp.where(i - nw + d >= 0, 0.0, MASKED)
        adds.append(add)
    s_near = _dot(tile(ks_ref, near, 3), qt) + (chosen(near, 3) + near_ref[...])
    s_win = _dot(tile(kw_ref, i, nw + 1), qt) + jnp.concatenate(adds, axis=0)
    m_old = m_ref[...]
    m_new = jnp.maximum(m_old, jnp.max(s_near, axis=0, keepdims=True))
    p_near = jnp.exp(s_near - m_new)
    pv_near = _dot_tn(tile(vs_ref, near, 3), _bf(p_near))
    p_win = jnp.exp(s_win - jnp.max(s_win, axis=0, keepdims=True))
    pv_win = _dot_tn(tile(vw_ref, i, nw + 1), _bf(p_win))
    alpha = jnp.exp(m_old - m_new)
    l_sel = alpha * l_ref[...] + jnp.sum(p_near, axis=0, keepdims=True)
    o_sel = (alpha * acc_ref[...] + pv_near) / jnp.maximum(l_sel, 1e-30)
    o_win = pv_win / jnp.maximum(jnp.sum(p_win, axis=0, keepdims=True), 1e-30)

    gate = gate_ref[...]
    for h in range(nh):
        sl = slice(h * NSA_DH, (h + 1) * NSA_DH)
        g_sel = _sigmoid(gate[:, nh + h:nh + h + 1])
        g_win = _sigmoid(gate[:, 2 * nh + h:2 * nh + h + 1])
        o_ref[:, sl] = _bf(ocmp_ref[:, sl] + g_sel * o_sel[:, sl].T + g_win * o_win[:, sl].T)


def nsa_sel_win_attention(qn, ks, vs, kw, vw, sel, near_add, win_add, proj, ocmp, batch, seq):
    nq = seq // NSA_QBLOCK
    wide = NSA_HEADS * NSA_QBLOCK
    padded = lambda t: pl.BlockSpec((None, t.shape[1], NSA_DH), lambda b, i: (b, 0, 0))
    table = lambda t: pl.BlockSpec(t.shape, lambda b, i: (0, 0))
    rows = lambda width, cb=0: pl.BlockSpec((NSA_QBLOCK, width), lambda b, i: (b * nq + i, cb))
    return pl.pallas_call(
        _nsa_sel_win_kernel,
        grid=(batch, nq),
        in_specs=[rows(512), padded(ks), padded(vs), padded(kw), padded(vw), rows(LANE), table(near_add),
                  table(win_add), rows(LANE, P_NSA_GATE // LANE), rows(512)],
        out_specs=rows(512),
        out_shape=jax.ShapeDtypeStruct((batch * seq, 512), BF16),
        scratch_shapes=[pltpu.VMEM((8 + LANE, wide), F32), pltpu.VMEM((1, wide), F32), pltpu.VMEM((1, wide), F32),
                        pltpu.VMEM((NSA_DH, wide), F32)],
        compiler_params=_cparams("parallel", "arbitrary"),
        name="nsa_sel_win_attention",
    )(qn, ks, vs, kw, vw, sel, near_add, win_add, proj, ocmp)


def _t5_bucket_np(dist):
    n = np.maximum(dist, 0)
    max_exact = NUM_BUCKETS // 2
    nf = np.maximum(n, 1).astype(np.float64)
    large = max_exact + (np.log(nf / max_exact) / math.log(MAX_DISTANCE / max_exact)
                         * (NUM_BUCKETS - max_exact)).astype(np.int64)
    large = np.minimum(large, NUM_BUCKETS - 1)
    return np.where(n < max_exact, n, large).astype(np.int32)


def _bias_tables(rel_bias):
    qb = NSA_QBLOCK
    r = np.arange(qb)[:, None]
    l = np.arange(qb)[None, :]
    toep_idx = np.stack([_t5_bucket_np(r - l), _t5_bucket_np(qb + r - l)])
    def lookup(idx):
        onehot = (jnp.asarray(idx.reshape(-1, 1)) == jnp.arange(NUM_BUCKETS)[None, :]).astype(F32)
        return jnp.dot(onehot, rel_bias.astype(F32), precision=HI).reshape(idx.shape + (NSA_HEADS,))

    toep = jnp.transpose(lookup(toep_idx), (3, 0, 1, 2))
    per = qb // NSA_CMP_STRIDE
    band_idx = np.zeros((16, qb, LANE), np.int32)
    for im in range(16):
        base = per * im - 9
        n = base + ((np.arange(LANE) - base) % LANE)
        dist = qb * im + r - NSA_CMP_STRIDE * n[None, :] - (NSA_CMP_LEN - 1)
        band_idx[im] = _t5_bucket_np(dist)
    band = jnp.transpose(lookup(band_idx), (0, 3, 1, 2))
    cfar = rel_bias[NUM_BUCKETS - 1]
    near = jnp.transpose(toep - cfar[:, None, None, None], (1, 3, 0, 2)).reshape(2, qb, NSA_HEADS * qb)
    key = np.arange(qb)[:, None]
    qry = np.tile(np.arange(qb), NSA_HEADS)[None, :]
    diag = jnp.where(key <= qry, near[0], MASKED)
    edge = jnp.asarray(np.where(qry < key, 0.0, MASKED), F32)
    zero = jnp.zeros_like(diag)
    near_add = jnp.concatenate([zero, near[1], diag], axis=0)
    win_add = jnp.concatenate([edge, zero, zero, near[1], diag], axis=0)
    return near_add.astype(F32), win_add.astype(F32), band.astype(F32), cfar.astype(F32)


def _overlap_table(nc, seq):
    n_cmp = (seq - NSA_CMP_LEN) // NSA_CMP_STRIDE + 1
    n_sel = seq // NSA_SEL_BLOCK
    cs = np.arange(nc)[:, None] * NSA_CMP_STRIDE
    ss = np.arange(LANE)[None, :] * NSA_SEL_BLOCK
    ovl = (cs < ss + NSA_SEL_BLOCK) & (cs + NSA_CMP_LEN > ss)
    ovl &= (np.arange(nc)[:, None] < n_cmp) & (np.arange(LANE)[None, :] < n_sel)
    return jnp.asarray(ovl.astype(np.float32), dtype=BF16)


def _pad_axis(t, axis, size):
    pad = [(0, 0)] * t.ndim
    pad[axis] = (0, size - t.shape[axis])
    return jnp.pad(t, pad)


def _pack_w_in(w_in):
    nl, k, _ = w_in.shape
    src = {}
    start = 0
    names = ("gla_q", "gla_k", "gla_v", "gla_g", "gla_lr", "lru_x", "lru_g", "nsa_q", "nsa_kv", "nsa_gate", "rwkv",
             "gates")
    widths = (256, 256, 512, 512, GLA_LOWRANK, 512, 512, 512, 768, 12, 1984, N_BRANCH * D_MODEL)
    for name, wd in zip(names, widths):
        src[name] = (start, wd)
        start += wd
    def piece(name, width):
        s, wd = src[name]
        return _pad_axis(w_in[:, :, s:s + wd], 2, width)
    rs, _ = src["rwkv"]
    rw = jnp.concatenate([
        w_in[:, :, rs:rs + 1536],
        _pad_axis(w_in[:, :, rs + 1536:rs + 1632], 2, LANE),
        _pad_axis(w_in[:, :, rs + 1632:rs + 1728], 2, LANE),
        w_in[:, :, rs + 1728:rs + 1984]], axis=2)
    packed = jnp.concatenate([
        piece("gates", P_GATES_TOTAL), rw, piece("lru_x", 512), piece("lru_g", 512), piece("nsa_q", 512),
        piece("gla_v", 512), piece("gla_g", 512), piece("nsa_kv", 768), piece("gla_q", 256), piece("gla_k", 256),
        piece("gla_lr", LANE), piece("nsa_gate", LANE)], axis=2)
    assert packed.shape[2] == P_GATES_TOTAL + P_MAIN
    return packed.astype(BF16)


def _pack_rwkv_vec(t):
    return jnp.concatenate([t[:, :1536], _pad_axis(t[:, 1536:1632], 1, LANE), _pad_axis(t[:, 1632:1728], 1, LANE),
                            t[:, 1728:1984]], axis=1)[:, None, :]


def _row(t):
    return t[:, None, :]


def kernel(x, rel_bias, attn_norm, ffn_norm, w_in, gla_w_gk, gla_b_gk, gla_out_norm, lru_conv_w, lru_conv_b, lru_w_a, lru_b_a, lru_w_i, lru_b_i, lru_lambda, nsa_cmp_pos, nsa_cmp_k1, nsa_cmp_k2, nsa_cmp_v1, nsa_cmp_v2, nsa_q_norm, nsa_k_norm, rwkv_mu, rwkv_w0, rwkv_w_lora, rwkv_a0, rwkv_a_lora, rwkv_g_lora, rwkv_k_k, rwkv_k_a, rwkv_r_k, rwkv_ln_w, rwkv_ln_b, w_branch, w_out, ffn_up, ffn_conv_w, ffn_conv_b, ffn_down):
    batch, seq, d = x.shape
    depth = w_in.shape[0]
    m = batch * seq
    nc = seq // NSA_CMP_STRIDE
    assert d == D_MODEL and seq % 2048 == 0

    w_in_p = _pack_w_in(w_in)
    w_branch_b = _bf(w_branch)
    w_out_b = _bf(w_out)
    ffn_up_b = _bf(ffn_up)
    ffn_down_b = _bf(ffn_down)
    gla_w_gk_p = _pad_axis(gla_w_gk, 1, LANE)
    lru_conv_w_p = _pad_axis(lru_conv_w, 1, 8)
    ffn_conv_w_p = _pad_axis(ffn_conv_w, 1, 8)
    rwkv_w_lora_p = _bf(_pad_axis(rwkv_w_lora, 1, LANE))
    rwkv_a_lora_p = _bf(_pad_axis(rwkv_a_lora, 1, LANE))
    cmp_pos_p = jnp.broadcast_to(nsa_cmp_pos.reshape(depth, 1, NSA_CMP_LEN * NSA_DH), (depth, 8, NSA_CMP_LEN * NSA_DH))
    near_add, win_add, band, cfar = _bias_tables(rel_bias)
    ovl = _overlap_table(nc, seq)

    tm = min(1024, seq)
    xf = x.reshape(m, d)
    for l in range(depth):
        gates, proj = norm_matmul(xf, _row(attn_norm), w_in_p, l, tm, 512)
        y_a = gla_mixer(proj, gla_w_gk_p, _row(gla_b_gk), _row(gla_out_norm), l, batch, seq)
        y_b = lru_mixer(proj, lru_conv_w_p, _row(lru_conv_b), _bf(lru_w_a), _row(lru_b_a), _bf(lru_w_i), _row(lru_b_i),
                        _row(lru_lambda), l, batch, seq, min(256, seq))
        qn, ks, vs, kw, vw = nsa_prep(proj, _row(nsa_q_norm), _row(nsa_k_norm), l, min(512, seq))
        kgrp = proj[:, P_NSA_KV:P_NSA_KV + NSA_DH].reshape(batch, nc, NSA_CMP_STRIDE * NSA_DH)
        vgrp = proj[:, P_NSA_KV + NSA_DH:P_NSA_KV + 2 * NSA_DH].reshape(batch, nc, NSA_CMP_STRIDE * NSA_DH)
        kc, vc = nsa_compress(kgrp, vgrp, cmp_pos_p, _bf(nsa_cmp_k1), _bf(nsa_cmp_k2), _bf(nsa_cmp_v1),
                              _bf(nsa_cmp_v2), _row(nsa_k_norm), l)
        ocmp, sel = nsa_cmp_attention(cfar, qn, kc, vc, band, ovl, proj, batch, seq)
        front = lambda t, tiles: jnp.pad(t.reshape(batch, seq, NSA_DH), ((0, 0), (tiles * NSA_QBLOCK, 0), (0, 0)))
        y_c = nsa_sel_win_attention(qn, front(ks, SEL_PAD_TILES), front(vs, SEL_PAD_TILES), front(kw, WIN_PAD_TILES),
                                    front(vw, WIN_PAD_TILES), sel, near_add, win_add, proj, ocmp, batch, seq)
        y_d = rwkv_mixer(proj, _pack_rwkv_vec(rwkv_mu), _row(rwkv_w0), rwkv_w_lora_p, _row(rwkv_a0), rwkv_a_lora_p,
                         _bf(rwkv_g_lora), _row(rwkv_k_k), _row(rwkv_k_a), _row(rwkv_r_k.reshape(depth, RWKV_WIDTH)),
                         _row(rwkv_ln_w), _row(rwkv_ln_b), l, batch, seq)
        merged = merge_branches((y_a, y_b, y_c, y_d), w_branch_b, gates, l, min(512, seq), 512)
        xf = matmul_residual(merged, w_out_b, xf, l, tm, 512)
        act = ffn_up_conv(xf, _row(ffn_norm), ffn_up_b, ffn_conv_w_p, _row(ffn_conv_b), l, seq, tm, 512)
        xf = matmul_residual(act, ffn_down_b, xf, l, tm, 512)
    return xf.reshape(batch, seq, d)
```

```python
import functools
import math

import numpy as np
import jax
import jax.numpy as jnp
from jax import lax
from jax.experimental import pallas as pl
from jax.experimental.pallas import tpu as pltpu

F32 = jnp.float32
BF16 = jnp.bfloat16
HI = lax.Precision.HIGHEST

LANE = 128
VMEM_LIMIT = 56 * 1024 * 1024

D_MODEL = 2048
N_BRANCH = 4
BRANCH_WIDTH = 512

GLA_HEADS, GLA_DK, GLA_DV, GLA_LOWRANK, GLA_NORMALIZER = 4, 64, 128, 16, 16.0
LRU_WIDTH, LRU_BLOCKS, LRU_CONV, LRU_C = 512, 4, 4, 8.0
NSA_HEADS, NSA_DH = 4, 128
NSA_CMP_LEN, NSA_CMP_STRIDE, NSA_SEL_BLOCK, NSA_SEL_TOPK, NSA_WINDOW, NSA_QBLOCK = 32, 16, 64, 16, 512, 128
RWKV_HEADS, RWKV_DH, RWKV_WIDTH = 8, 64, 512
RWKV_W_LORA, RWKV_A_LORA, RWKV_G_LORA = 96, 96, 256
NUM_BUCKETS, MAX_DISTANCE = 32, 128
D_FF, FFN_CONV = 5632, 3
NEG_BIG = 1e9
MASKED = -1e30
M_INIT = -1e20

P_RWKV, P_LRU_X, P_LRU_G, P_NSA_Q, P_GLA_V, P_GLA_G = 0, 2048, 2560, 3072, 3584, 4096
P_NSA_KV, P_GLA_Q, P_GLA_K, P_GLA_LR, P_NSA_GATE, P_MAIN = 4608, 5376, 5632, 5888, 6016, 6144
P_GATES_TOTAL = 8192
RW_R, RW_K, RW_V, RW_XW, RW_XA, RW_XG, RW_TOTAL = 0, 512, 1024, 1536, 1664, 1792, 2048

CHUNK = 64


def _cparams(*sem):
    return pltpu.CompilerParams(dimension_semantics=sem, vmem_limit_bytes=VMEM_LIMIT)


def _dot(a, b, prec=None):
    return jnp.dot(a, b, preferred_element_type=F32, precision=prec)


def _dot_nt(a, b, prec=None):
    return lax.dot_general(a, b, (((1,), (1,)), ((), ())), preferred_element_type=F32, precision=prec)


def _dot_tn(a, b, prec=None):
    return lax.dot_general(a, b, (((0,), (0,)), ((), ())), preferred_element_type=F32, precision=prec)


def _bf(x):
    return x.astype(BF16)


def _sigmoid(x):
    return 1.0 / (1.0 + jnp.exp(-x))


def _log_sigmoid(x):
    return jnp.minimum(x, 0.0) - jnp.log(1.0 + jnp.exp(-jnp.abs(x)))


def _gelu_tanh(x):
    return 0.5 * x * (1.0 + jnp.tanh(math.sqrt(2.0 / math.pi) * (x + 0.044715 * (x * x * x))))


def _iota(shape, dim):
    return lax.broadcasted_iota(jnp.int32, shape, dim)


def _idiv(x, d):
    assert d & (d - 1) == 0
    return jnp.right_shift(x, d.bit_length() - 1)


def _shift_rows(x, s, top):
    xr = pltpu.roll(x, s, 0)
    tr = pltpu.roll(top, s, 0)
    row = _iota((8, x.shape[1]), 0)
    head = jnp.where(row < s, tr, xr[:8])
    return jnp.concatenate([head, xr[8:]], axis=0)


def _norm_mm_kernel(x_ref, g_ref, w_ref, og_ref, om_ref, xn_ref, *, gate_tiles):
    j = pl.program_id(1)

    @pl.when(j == 0)
    def _():
        x = x_ref[...]
        ms = jnp.mean(x * x, axis=-1, keepdims=True)
        xn_ref[...] = _bf(x * lax.rsqrt(ms + 1e-6) * g_ref[...])

    @pl.when(j < gate_tiles)
    def _():
        og_ref[...] = _bf(_dot(xn_ref[...], w_ref[...]))

    @pl.when(j >= gate_tiles)
    def _():
        om_ref[...] = _dot(xn_ref[...], w_ref[...])


def norm_matmul(x, gain, w, layer, tm, tn):
    m, k = x.shape
    n = w.shape[2]
    gate_tiles = P_GATES_TOTAL // tn
    kern = functools.partial(_norm_mm_kernel, gate_tiles=gate_tiles)
    return pl.pallas_call(
        kern,
        grid=(m // tm, n // tn),
        in_specs=[pl.BlockSpec((tm, k), lambda i, j: (i, 0)),
                  pl.BlockSpec((None, 1, k), lambda i, j: (layer, 0, 0)),
                  pl.BlockSpec((None, k, tn), lambda i, j: (layer, 0, j))],
        out_specs=[pl.BlockSpec((tm, tn), lambda i, j: (i, jnp.minimum(j, gate_tiles - 1))),
                   pl.BlockSpec((tm, tn), lambda i, j: (i, jnp.maximum(j - gate_tiles, 0)))],
        out_shape=[jax.ShapeDtypeStruct((m, P_GATES_TOTAL), BF16), jax.ShapeDtypeStruct((m, n - P_GATES_TOTAL), F32)],
        scratch_shapes=[pltpu.VMEM((tm, k), BF16)],
        compiler_params=_cparams("parallel", "arbitrary"),
        name="norm_matmul",
    )(x, gain, w)


def _mm_res_kernel(a_ref, w_ref, r_ref, o_ref):
    o_ref[...] = r_ref[...] + _dot(a_ref[...], w_ref[...])


def matmul_residual(a, w, res, layer, tm, tn):
    m, k = a.shape
    n = w.shape[2]
    return pl.pallas_call(
        _mm_res_kernel,
        grid=(m // tm, n // tn),
        in_specs=[pl.BlockSpec((tm, k), lambda i, j: (i, 0)),
                  pl.BlockSpec((None, k, tn), lambda i, j: (layer, 0, j)),
                  pl.BlockSpec((tm, tn), lambda i, j: (i, j))],
        out_specs=pl.BlockSpec((tm, tn), lambda i, j: (i, j)),
        out_shape=jax.ShapeDtypeStruct((m, n), F32),
        compiler_params=_cparams("parallel", "arbitrary"),
        name="matmul_residual",
    )(a, w, res)


def _merge_kernel(ya_ref, yb_ref, yc_ref, yd_ref, wb_ref, g0_ref, g1_ref, g2_ref, g3_ref, o_ref):
    acc = None
    for n, (y_ref, g_ref) in enumerate(((ya_ref, g0_ref), (yb_ref, g1_ref), (yc_ref, g2_ref), (yd_ref, g3_ref))):
        t = _sigmoid(g_ref[...].astype(F32)) * _dot(y_ref[...], wb_ref[n])
        acc = t if acc is None else acc + t
    o_ref[...] = _bf(acc)


def merge_branches(ys, w_branch, gates, layer, tm, tn):
    m = ys[0].shape[0]
    nj = D_MODEL // tn
    y_spec = pl.BlockSpec((tm, BRANCH_WIDTH), lambda i, j: (i, 0))
    gate_specs = [pl.BlockSpec((tm, tn), functools.partial(lambda i, j, n: (i, (n * D_MODEL) // tn + j), n=n))
                  for n in range(N_BRANCH)]
    return pl.pallas_call(
        _merge_kernel,
        grid=(m // tm, nj),
        in_specs=[y_spec] * 4 + [pl.BlockSpec((None, N_BRANCH, BRANCH_WIDTH, tn), lambda i, j: (layer, 0, 0, j))]
        + gate_specs,
        out_specs=pl.BlockSpec((tm, tn), lambda i, j: (i, j)),
        out_shape=jax.ShapeDtypeStruct((m, D_MODEL), BF16),
        compiler_params=_cparams("parallel", "arbitrary"),
        name="merge_branches",
    )(*ys, w_branch, gates, gates, gates, gates)


def _ffn_up_kernel(x_ref, g_ref, wg_ref, wv_ref, cw_g_ref, cw_v_ref, cb_g_ref, cb_v_ref, o_ref,
                   xn_ref, carry_ref, *, tiles_per_seq):
    i, j = pl.program_id(0), pl.program_id(1)

    @pl.when(j == 0)
    def _():
        x = x_ref[...]
        ms = jnp.mean(x * x, axis=-1, keepdims=True)
        xn_ref[...] = _bf(x * lax.rsqrt(ms + 1e-6) * g_ref[...])

    first = (i % tiles_per_seq) == 0

    @pl.when(first)
    def _():
        carry_ref[j] = jnp.zeros(carry_ref.shape[1:], F32)

    xn = xn_ref[...]
    outs = []
    for half, (w_ref, cw_ref, cb_ref) in enumerate(((wg_ref, cw_g_ref, cb_g_ref), (wv_ref, cw_v_ref, cb_v_ref))):
        u = _dot(xn, w_ref[...])
        top = carry_ref[j, half]
        carry_ref[j, half] = u[u.shape[0] - 8:]
        cw = cw_ref[...]
        outs.append(cw[0:1] * _shift_rows(u, 2, top) + cw[1:2] * _shift_rows(u, 1, top) + cw[2:3] * u + cb_ref[...])
    gate, val = outs
    o_ref[...] = _bf(gate * _sigmoid(gate) * val)


def ffn_up_conv(x, gain, w_up, conv_w, conv_b, layer, seq, tm, tn):
    m, k = x.shape
    nj = D_FF // tn
    kern = functools.partial(_ffn_up_kernel, tiles_per_seq=seq // tm)
    return pl.pallas_call(
        kern,
        grid=(m // tm, nj),
        in_specs=[pl.BlockSpec((tm, k), lambda i, j: (i, 0)),
                  pl.BlockSpec((None, 1, k), lambda i, j: (layer, 0, 0)),
                  pl.BlockSpec((None, k, tn), lambda i, j: (layer, 0, j)),
                  pl.BlockSpec((None, k, tn), lambda i, j: (layer, 0, nj + j)),
                  pl.BlockSpec((None, 8, tn), lambda i, j: (layer, 0, j)),
                  pl.BlockSpec((None, 8, tn), lambda i, j: (layer, 0, nj + j)),
                  pl.BlockSpec((None, 1, tn), lambda i, j: (layer, 0, j)),
                  pl.BlockSpec((None, 1, tn), lambda i, j: (layer, 0, nj + j))],
        out_specs=pl.BlockSpec((tm, tn), lambda i, j: (i, j)),
        out_shape=jax.ShapeDtypeStruct((m, D_FF), BF16),
        scratch_shapes=[pltpu.VMEM((tm, k), BF16), pltpu.VMEM((nj, 2, 8, tn), F32)],
        compiler_params=_cparams("arbitrary", "arbitrary"),
        name="ffn_up",
    )(x, gain, w_up, w_up, conv_w, conv_w, conv_b, conv_b)


def _gla_kernel(q_ref, k_ref, v_ref, g_ref, lr_ref, wgk_ref, bgk_ref, gain_ref, o_ref, st_ref):
    c = CHUNK

    nb = q_ref.shape[0]
    npair = GLA_HEADS // 2

    @pl.when(pl.program_id(0) == 0)
    def _():
        st_ref[...] = jnp.zeros_like(st_ref)

    row = _iota((c, c), 0)
    col = _iota((c, c), 1)
    tri_incl = (col <= row).astype(F32)
    lane = _iota((1, LANE), 1)
    bd = _idiv(_iota((2 * GLA_DV, LANE), 0), GLA_DV) == _idiv(_iota((2 * GLA_DV, LANE), 1), GLA_DK)
    sub = 16
    nsub = c // sub
    scores = {}
    o_inter = {}
    vbs = []
    for bi in range(nb):
        log_a = _log_sigmoid(_dot(lr_ref[bi], wgk_ref[...], HI) + bgk_ref[...]) * (1.0 / GLA_NORMALIZER)
        b = _dot(tri_incl, log_a, HI)
        vb = _bf(v_ref[bi])
        vbs.append(vb)
        for p in range(npair):
            sl = slice(p * LANE, (p + 1) * LANE)
            qp = q_ref[bi, :, sl] * (GLA_DK ** -0.5)
            kp = k_ref[bi, :, sl]
            bp = b[:, sl]
            b_last = bp[c - 1:c]
            st = st_ref[bi * npair + p]
            o_inter[bi, p] = _dot_nt(_bf(qp * jnp.exp(bp)), _bf(st))
            upd = _dot_tn(vb[:, 2 * p * GLA_DV:(2 * p + 2) * GLA_DV], _bf(kp * jnp.exp(b_last - bp)))
            st_ref[bi * npair + p] = st * jnp.exp(b_last) + jnp.where(bd, upd, 0.0)
            for i in range(nsub):
                rows = slice(i * sub, (i + 1) * sub)
                nrow = (i + 1) * sub
                bref = bp[i * sub - 1:i * sub] if i > 0 else jnp.zeros((1, LANE), F32)
                qi = qp[rows] * jnp.exp(bp[rows] - bref)
                ki = _bf(kp[:nrow] * jnp.exp(jnp.minimum(bref - bp[:nrow], 80.0)))
                causal = _iota((sub, nrow), 1) <= (_iota((sub, nrow), 0) + i * sub)
                for hh in range(2):
                    head_lanes = _idiv(lane, GLA_DK) == hh
                    s = _dot_nt(_bf(jnp.where(head_lanes, qi, 0.0)), ki)
                    scores[bi, 2 * p + hh, i] = _bf(jnp.where(causal, s, 0.0))
    gain = gain_ref[...]
    for bi in range(nb):
        for h in range(GLA_HEADS):
            hs = slice(h * GLA_DV, (h + 1) * GLA_DV)
            intra = [_dot(scores[bi, h, i], vbs[bi][:(i + 1) * sub, hs]) for i in range(nsub)]
            hh = h % 2
            o = o_inter[bi, h // 2][:, hh * GLA_DV:(hh + 1) * GLA_DV] + jnp.concatenate(intra, axis=0)
            y = o * lax.rsqrt(jnp.mean(o * o, axis=-1, keepdims=True) + 1e-6) * gain
            g = g_ref[bi, :, hs]
            o_ref[bi, :, hs] = _bf(y * (g * _sigmoid(g)))


def gla_mixer(proj, w_gk, b_gk, out_gain, layer, batch, seq):
    nt = seq // CHUNK
    def col(off, width):
        return pl.BlockSpec((batch, CHUNK, width), lambda t: (0, t, off // width))
    proj3 = proj.reshape(batch, seq, proj.shape[1])
    out = pl.pallas_call(
        _gla_kernel,
        grid=(nt,),
        in_specs=[col(P_GLA_Q, 256), col(P_GLA_K, 256), col(P_GLA_V, 512), col(P_GLA_G, 512), col(P_GLA_LR, 128),
                  pl.BlockSpec((None, LANE, 256), lambda t: (layer, 0, 0)),
                  pl.BlockSpec((None, 1, 256), lambda t: (layer, 0, 0)),
                  pl.BlockSpec((None, 1, GLA_DV), lambda t: (layer, 0, 0))],
        out_specs=pl.BlockSpec((batch, CHUNK, 512), lambda t: (0, t, 0)),
        out_shape=jax.ShapeDtypeStruct((batch, seq, 512), BF16),
        scratch_shapes=[pltpu.VMEM((batch * (GLA_HEADS // 2), 2 * GLA_DV, LANE), F32)],
        compiler_params=_cparams("arbitrary"),
        name="gla_mixer",
    )(proj3, proj3, proj3, proj3, proj3, w_gk, b_gk, out_gain)
    return out.reshape(batch * seq, 512)


def _lru_kernel(x_ref, gb_ref, cw_ref, cb_ref, wa_ref, ba_ref, wi_ref, bi_ref, lam_ref, o_ref, xc_ref, h_ref):
    t = x_ref.shape[0]

    @pl.when(pl.program_id(1) == 0)
    def _():
        xc_ref[...] = jnp.zeros_like(xc_ref)
        h_ref[...] = jnp.zeros_like(h_ref)

    x = x_ref[...]
    top = xc_ref[...]
    xc_ref[...] = x[t - 8:]
    cw = cw_ref[...]
    xc = (cw[0:1] * _shift_rows(x, 3, top) + cw[1:2] * _shift_rows(x, 2, top) + cw[2:3] * _shift_rows(x, 1, top)
          + cw[3:4] * x + cb_ref[...])
    xcb = _bf(xc)
    ra, ri = [], []
    for n in range(LRU_BLOCKS):
        blk = xcb[:, n * LANE:(n + 1) * LANE]
        ra.append(_dot(blk, wa_ref[n]))
        ri.append(_dot(blk, wi_ref[n]))
    r = _sigmoid(jnp.concatenate(ra, axis=1) + ba_ref[...])
    gi = _sigmoid(jnp.concatenate(ri, axis=1) + bi_ref[...])
    log_a = LRU_C * r * _log_sigmoid(lam_ref[...])
    a = jnp.exp(log_a)
    u = jnp.sqrt(-jnp.tanh(log_a) * (a * a + 1.0)) * (gi * xc)
    row = _iota((t, LRU_WIDTH), 0)
    k = 1
    while k < t:
        a_sh = jnp.where(row < k, 1.0, pltpu.roll(a, k, 0))
        u_sh = jnp.where(row < k, 0.0, pltpu.roll(u, k, 0))
        u = u + a * u_sh
        a = a * a_sh
        k *= 2
    h = a * h_ref[0:1] + u
    h_ref[...] = jnp.broadcast_to(h[t - 1:t], h_ref.shape)
    o_ref[...] = _bf(h * _gelu_tanh(gb_ref[...]))


def lru_mixer(proj, conv_w, conv_b, w_a, b_a, w_i, b_i, lam, layer, batch, seq, tt):
    nt = seq // tt
    vec = pl.BlockSpec((None, 1, LRU_WIDTH), lambda b, t: (layer, 0, 0))
    wblk = pl.BlockSpec((None, LRU_BLOCKS, LANE, LANE), lambda b, t: (layer, 0, 0, 0))
    return pl.pallas_call(
        _lru_kernel,
        grid=(batch, nt),
        in_specs=[pl.BlockSpec((tt, 512), lambda b, t: (b * nt + t, P_LRU_X // 512)),
                  pl.BlockSpec((tt, 512), lambda b, t: (b * nt + t, P_LRU_G // 512)),
                  pl.BlockSpec((None, 8, LRU_WIDTH), lambda b, t: (layer, 0, 0)), vec, wblk, vec, wblk, vec, vec],
        out_specs=pl.BlockSpec((tt, 512), lambda b, t: (b * nt + t, 0)),
        out_shape=jax.ShapeDtypeStruct((batch * seq, 512), BF16),
        scratch_shapes=[pltpu.VMEM((8, LRU_WIDTH), F32), pltpu.VMEM((8, LRU_WIDTH), F32)],
        compiler_params=_cparams("parallel", "arbitrary"),
        name="lru_mixer",
    )(proj, proj, conv_w, conv_b, w_a, b_a, w_i, b_i, lam)


def _rwkv_kernel(f_ref, mu_ref, w0_ref, wl_ref, a0_ref, al_ref, gl_ref, kk_ref, ka_ref, rk_ref, lnw_ref, lnb_ref,
                 o_ref, st_ref, prev_ref):
    c = CHUNK
    n = RWKV_DH

    nb = f_ref.shape[0]

    @pl.when(pl.program_id(0) == 0)
    def _():
        st_ref[...] = jnp.zeros_like(st_ref)
        prev_ref[...] = jnp.zeros_like(prev_ref)

    feats = [f_ref[b] for b in range(nb)]
    prev = jnp.concatenate([_shift_rows(feats[b], 1, prev_ref[b]) for b in range(nb)], axis=0)
    for b in range(nb):
        prev_ref[b] = feats[b][c - 8:]
    feat = jnp.concatenate(feats, axis=0)
    xm = feat + (prev - feat) * mu_ref[...]
    r = xm[:, RW_R:RW_R + 512]
    k = xm[:, RW_K:RW_K + 512]
    v = xm[:, RW_V:RW_V + 512]
    log_w = -math.exp(-0.5) * _sigmoid(w0_ref[...] + _dot(_bf(jnp.tanh(xm[:, RW_XW:RW_XW + LANE])), wl_ref[...]))
    a = _sigmoid(a0_ref[...] + _dot(_bf(xm[:, RW_XA:RW_XA + LANE]), al_ref[...]))
    g = _dot(_bf(_sigmoid(xm[:, RW_XG:RW_XG + 256])), gl_ref[...])

    assert c == n and 2 * n == LANE
    head0 = _iota((1, LANE), 1) < n
    same_head = _idiv(_iota((LANE, LANE), 0), n) == _idiv(_iota((LANE, LANE), 1), n)
    head_ones = _bf(same_head.astype(F32))
    row = _iota((c, LANE), 0)
    col = jnp.bitwise_and(_iota((c, LANE), 1), n - 1)
    tri_incl = col <= row
    tri_strict = col < row
    rr_ = _iota((nb * c, nb * c), 0)
    cc_ = _iota((nb * c, nb * c), 1)
    tri_sq = _bf(((cc_ <= rr_) & (_idiv(cc_, c) == _idiv(rr_, c))).astype(F32))

    def stack(t):
        return jnp.concatenate([jnp.where(head0, t, 0.0), jnp.where(head0, 0.0, t)], axis=0)

    def seg_sum(t):
        hi = _bf(t)
        return _dot(hi, head_ones) + _dot(_bf(t - hi.astype(F32)), head_ones)

    lw_hi = _bf(log_w)
    lw_r = log_w - lw_hi.astype(F32)
    lw_mid = _bf(lw_r)
    cum_all = _dot(tri_sq, lw_hi) + _dot(tri_sq, lw_mid) + _dot(tri_sq, _bf(lw_r - lw_mid.astype(F32)))

    pairs = range(RWKV_HEADS // 2)
    sls = [slice(p * LANE, (p + 1) * LANE) for p in pairs]
    kk = [k[:, sl] * kk_ref[:, sl] for sl in sls]
    ss = [seg_sum(t * t) for t in kk]
    kk = [t / jnp.maximum(jnp.sqrt(s), 1e-12) for t, s in zip(kk, ss)]
    k2 = [k[:, sl] * (1.0 + (a[:, sl] - 1.0) * ka_ref[:, sl]) for sl in sls]
    e_neg = [jnp.exp(-cum_all[:, sl]) for sl in sls]
    rt_all = [_bf(r[:, sl] * jnp.exp(cum_all[:, sl])) for sl in sls]
    kt_all = [_bf(kk[p] * jnp.exp(cum_all[:, sls[p]] - log_w[:, sls[p]])) for p in pairs]
    kb_all = [k2[p] * e_neg[p] for p in pairs]
    bb_all = [kk[p] * a[:, sls[p]] * e_neg[p] for p in pairs]
    chains = [(b, p) for b in range(nb) for p in pairs]
    nch = range(len(chains))
    rows = [slice(b * c, (b + 1) * c) for b, _ in chains]
    e_last = [jnp.exp(cum_all[b * c + c - 1:(b + 1) * c, sls[p]]) for b, p in chains]
    rt = [rt_all[p][rows[i]] for i, (_, p) in enumerate(chains)]
    kt = [kt_all[p][rows[i]] for i, (_, p) in enumerate(chains)]
    kb = [kb_all[p][rows[i]] for i, (_, p) in enumerate(chains)]
    bb = [bb_all[p][rows[i]] for i, (_, p) in enumerate(chains)]
    vv = [v[rows[i], sls[p]] for i, (_, p) in enumerate(chains)]
    kb2 = [_bf(stack(t)) for t in kb]
    bb2 = [_bf(stack(t)) for t in bb]
    vstk = [_bf(stack(t)) for t in vv]
    st = [st_ref[i] for i in nch]
    stb = [_bf(t) for t in st]
    a_mat = [jnp.where(tri_strict, _dot_nt(kt[i], bb2[i]), 0.0) for i in nch]
    b_mat = [jnp.where(tri_strict, _dot_nt(kt[i], kb2[i]), 0.0) for i in nch]
    rr_mat = [jnp.where(tri_incl, _dot_nt(rt[i], kb2[i]), 0.0) for i in nch]
    rb_mat = [jnp.where(tri_incl, _dot_nt(rt[i], bb2[i]), 0.0) for i in nch]
    rhs = [_dot_nt(kt[i], stb[i]) + _dot(_bf(b_mat[i]), vstk[i]) for i in nch]
    y0 = [_dot_nt(rt[i], stb[i]) + _dot(_bf(rr_mat[i]), vstk[i]) for i in nch]
    nmat = [-t for t in a_mat]
    tm1 = list(nmat)
    pw = 1
    while 2 * pw < c:
        nmat = [_dot(_bf(t), _bf(stack(t))) for t in nmat]
        tm1 = [tm1[i] + nmat[i] + _dot(_bf(tm1[i]), _bf(stack(nmat[i]))) for i in nch]
        pw *= 2
    u = [rhs[i] + _dot(_bf(tm1[i]), _bf(stack(rhs[i]))) for i in nch]
    y = [y0[i] - _dot(_bf(rb_mat[i]), _bf(stack(u[i]))) for i in nch]
    upd = [_dot_tn(_bf(jnp.concatenate([vv[i], -u[i]], axis=0)),
                   _bf(jnp.concatenate([kb[i], bb[i]], axis=0) * e_last[i])) for i in nch]
    for i in nch:
        st_ref[i] = st[i] * e_last[i] + jnp.where(same_head, upd[i], 0.0)
    y_all = [jnp.concatenate([y[b * len(pairs) + p] for b in range(nb)], axis=0) for p in pairs]
    yc = [y_all[p] - seg_sum(y_all[p]) * (1.0 / n) for p in pairs]
    var = [seg_sum(t * t) * (1.0 / n) for t in yc]
    bonus = [seg_sum(r[:, sls[p]] * k2[p] * rk_ref[:, sls[p]]) for p in pairs]
    out = [(yc[p] * lax.rsqrt(var[p] + 64e-5) * lnw_ref[:, sls[p]] + lnb_ref[:, sls[p]] + bonus[p] * v[:, sls[p]])
           for p in pairs]
    out = _bf(jnp.concatenate(out, axis=1) * g)
    for b in range(nb):
        o_ref[b] = out[b * c:(b + 1) * c]


def rwkv_mixer(proj, mu, w0, w_lora, a0, a_lora, g_lora, k_k, k_a, r_k, ln_w, ln_b, layer, batch, seq):
    nt = seq // CHUNK
    vec = pl.BlockSpec((None, 1, RWKV_WIDTH), lambda t: (layer, 0, 0))
    out = pl.pallas_call(
        _rwkv_kernel,
        grid=(nt,),
        in_specs=[pl.BlockSpec((batch, CHUNK, RW_TOTAL), lambda t: (0, t, P_RWKV // RW_TOTAL)),
                  pl.BlockSpec((None, 1, RW_TOTAL), lambda t: (layer, 0, 0)),
                  vec, pl.BlockSpec((None, LANE, RWKV_WIDTH), lambda t: (layer, 0, 0)),
                  vec, pl.BlockSpec((None, LANE, RWKV_WIDTH), lambda t: (layer, 0, 0)),
                  pl.BlockSpec((None, RWKV_G_LORA, RWKV_WIDTH), lambda t: (layer, 0, 0)),
                  vec, vec, vec, vec, vec],
        out_specs=pl.BlockSpec((batch, CHUNK, 512), lambda t: (0, t, 0)),
        out_shape=jax.ShapeDtypeStruct((batch, seq, 512), BF16),
        scratch_shapes=[pltpu.VMEM((batch * (RWKV_HEADS // 2), LANE, LANE), F32),
                        pltpu.VMEM((batch, 8, RW_TOTAL), F32)],
        compiler_params=_cparams("arbitrary"),
        name="rwkv_mixer",
    )(proj.reshape(batch, seq, proj.shape[1]), mu, w0, w_lora, a0, a_lora, g_lora, k_k, k_a, r_k, ln_w, ln_b)
    return out.reshape(batch * seq, 512)


def _head_rms(x, gain):
    return x * lax.rsqrt(jnp.mean(x * x, axis=-1, keepdims=True) + 1e-6) * gain


def _nsa_prep_kernel(q_ref, kv_ref, qg_ref, kg_ref, qn_ref, ks_ref, vs_ref, kw_ref, vw_ref):
    qg = qg_ref[...] * (NSA_DH ** -0.5)
    for h in range(NSA_HEADS):
        sl = slice(h * NSA_DH, (h + 1) * NSA_DH)
        qn_ref[:, sl] = _bf(_head_rms(q_ref[:, sl], qg))
    kg = kg_ref[...]
    ks_ref[...] = _bf(_head_rms(kv_ref[:, 2 * NSA_DH:3 * NSA_DH], kg))
    vs_ref[...] = _bf(kv_ref[:, 3 * NSA_DH:4 * NSA_DH])
    kw_ref[...] = _bf(_head_rms(kv_ref[:, 4 * NSA_DH:5 * NSA_DH], kg))
    vw_ref[...] = _bf(kv_ref[:, 5 * NSA_DH:6 * NSA_DH])


def nsa_prep(proj, q_gain, k_gain, layer, tt):
    m = proj.shape[0]
    gain = pl.BlockSpec((None, 1, NSA_DH), lambda i: (layer, 0, 0))
    kv_out = pl.BlockSpec((tt, NSA_DH), lambda i: (i, 0))
    kv_shape = jax.ShapeDtypeStruct((m, NSA_DH), BF16)
    return pl.pallas_call(
        _nsa_prep_kernel,
        grid=(m // tt,),
        in_specs=[pl.BlockSpec((tt, 512), lambda i: (i, P_NSA_Q // 512)),
                  pl.BlockSpec((tt, 768), lambda i: (i, P_NSA_KV // 768)), gain, gain],
        out_specs=[pl.BlockSpec((tt, 512), lambda i: (i, 0)), kv_out, kv_out, kv_out, kv_out],
        out_shape=[jax.ShapeDtypeStruct((m, 512), BF16), kv_shape, kv_shape, kv_shape, kv_shape],
        compiler_params=_cparams("parallel"),
        name="nsa_prep",
    )(proj, proj, q_gain, k_gain)


def _nsa_compress_kernel(kg_ref, vg_ref, pos_ref, k1_ref, k2_ref, v1_ref, v2_ref, gain_ref, kc_ref, vc_ref):
    nc = kg_ref.shape[0]
    half = NSA_CMP_STRIDE * NSA_DH
    pos = _bf(pos_ref[...])

    def compress(g_ref, w1_ref, w2_ref):
        grp = _bf(g_ref[...])
        first = _dot(grp, w1_ref[:half])
        second = _dot(grp, w1_ref[half:])
        const = _dot(pos, w1_ref[...])[0:1]
        hid = first + pltpu.roll(second, nc - 1, 0) + const
        return _dot(_bf(_gelu_tanh(hid)), w2_ref[...])

    kc_ref[...] = _bf(_head_rms(compress(kg_ref, k1_ref, k2_ref), gain_ref[...]))
    vc_ref[...] = _bf(compress(vg_ref, v1_ref, v2_ref))


def nsa_compress(kgrp, vgrp, pos, k1, k2, v1, v2, k_gain, layer):
    batch, nc, width = kgrp.shape
    grp = pl.BlockSpec((None, nc, width), lambda b: (b, 0, 0))
    w1 = pl.BlockSpec((None, 2 * width, NSA_DH), lambda b: (layer, 0, 0))
    w2 = pl.BlockSpec((None, NSA_DH, NSA_DH), lambda b: (layer, 0, 0))
    out = pl.BlockSpec((None, nc, NSA_DH), lambda b: (b, 0, 0))
    shape = jax.ShapeDtypeStruct((batch, nc, NSA_DH), BF16)
    return pl.pallas_call(
        _nsa_compress_kernel,
        grid=(batch,),
        in_specs=[grp, grp, pl.BlockSpec((None, 8, 2 * width), lambda b: (layer, 0, 0)), w1, w2, w1, w2,
                  pl.BlockSpec((None, 1, NSA_DH), lambda b: (layer, 0, 0))],
        out_specs=[out, out],
        out_shape=[shape, shape],
        compiler_params=_cparams("parallel"),
        name="nsa_compress",
    )(kgrp, vgrp, pos, k1, k2, v1, v2, k_gain)


CMP_GROUP = 8


def _nsa_cmp_kernel(cfar_ref, q_ref, kc_ref, vc_ref, *rest):
    band_refs = rest[:CMP_GROUP]
    ovl_ref, gate_ref, o_ref, sel_ref = rest[CMP_GROUP:]
    qb = NSA_QBLOCK
    nc = kc_ref.shape[0]
    kc = kc_ref[...]
    vc = vc_ref[...]
    ovl = ovl_ref[...]
    r = _iota((qb, nc), 0)
    ncol = _iota((qb, nc), 1)
    blk = _iota((qb, LANE), 1)
    works = []
    for g in range(CMP_GROUP):
        i = pl.program_id(1) * CMP_GROUP + g
        rows = slice(g * qb, (g + 1) * qb)
        dist = qb * i + r - NSA_CMP_STRIDE * ncol - (NSA_CMP_LEN - 1)
        visible = dist >= 0
        lo = (qb // NSA_CMP_STRIDE) * i - 9
        in_band = (ncol >= lo) & (ncol <= lo + 15)
        gate = gate_ref[rows, :]
        p_sum = jnp.zeros((qb, nc), F32)
        for h in range(NSA_HEADS):
            sl = slice(h * NSA_DH, (h + 1) * NSA_DH)
            band = jnp.concatenate([band_refs[g][h]] * (nc // LANE), axis=1)
            logit = _dot_nt(q_ref[rows, sl], kc) + jnp.where(in_band, band, cfar_ref[h])
            logit = jnp.where(visible, logit, MASKED)
            mx = jnp.max(logit, axis=-1, keepdims=True)
            p = jnp.where(visible, jnp.exp(logit - mx), 0.0)
            p = p / jnp.maximum(jnp.sum(p, axis=-1, keepdims=True), 1e-30)
            p_sum = p_sum + p
            o_ref[rows, sl] = _sigmoid(gate[:, h:h + 1]) * _dot(_bf(p), vc)
        p_hi = _bf(p_sum)
        score = _dot(p_hi, ovl) + _dot(_bf(p_sum - p_hi.astype(F32)), ovl)
        pos = qb * i + _iota((qb, LANE), 0)
        cur = _idiv(pos, NSA_SEL_BLOCK)
        forced = (blk == 0) | (blk == cur) | (blk == cur - 1)
        works.append(jnp.where(forced, NEG_BIG, jnp.where(blk * NSA_SEL_BLOCK <= pos, score, -NEG_BIG)))
    work = jnp.concatenate(works, axis=0)
    sel = jnp.zeros(work.shape, F32)
    blk_f = _iota(work.shape, 1).astype(F32)
    for _ in range(NSA_SEL_TOPK):
        mx = jnp.max(work, axis=-1, keepdims=True)
        first = jnp.min(jnp.where(work == mx, blk_f, float(LANE)), axis=-1, keepdims=True)
        pick = blk_f == first
        sel = jnp.where(pick, 1.0, sel)
        work = jnp.where(pick, -jnp.inf, work)
    sel_ref[...] = _bf(sel)


def nsa_cmp_attention(cfar, qn, kc, vc, band, ovl, proj, batch, seq):
    rows = CMP_GROUP * NSA_QBLOCK
    ns = seq // rows
    nc = kc.shape[1]
    full = pl.BlockSpec((None, nc, NSA_DH), lambda b, i: (b, 0, 0))
    band_specs = [pl.BlockSpec((None, NSA_HEADS, NSA_QBLOCK, LANE),
                               functools.partial(lambda b, i, g: ((i * CMP_GROUP + g) % 16, 0, 0, 0), g=g))
                  for g in range(CMP_GROUP)]
    return pl.pallas_call(
        _nsa_cmp_kernel,
        grid=(batch, ns),
        in_specs=[pl.BlockSpec(memory_space=pltpu.SMEM),
                  pl.BlockSpec((rows, 512), lambda b, i: (b * ns + i, 0)), full, full] + band_specs
        + [pl.BlockSpec((nc, LANE), lambda b, i: (0, 0)),
           pl.BlockSpec((rows, LANE), lambda b, i: (b * ns + i, P_NSA_GATE // LANE))],
        out_specs=[pl.BlockSpec((rows, 512), lambda b, i: (b * ns + i, 0)),
                   pl.BlockSpec((rows, LANE), lambda b, i: (b * ns + i, 0))],
        out_shape=[jax.ShapeDtypeStruct((batch * seq, 512), F32), jax.ShapeDtypeStruct((batch * seq, LANE), BF16)],
        compiler_params=_cparams("parallel", "arbitrary"),
        name="nsa_cmp_attention",
    )(cfar, qn, kc, vc, *([band] * CMP_GROUP), ovl, proj)


FAR_GROUP = 4
SEL_PAD_TILES = FAR_GROUP
WIN_PAD_TILES = NSA_WINDOW // NSA_QBLOCK


def _nsa_sel_win_kernel(q_ref, ks_ref, vs_ref, kw_ref, vw_ref, sel_ref, near_ref, win_ref, gate_ref, ocmp_ref,
                        o_ref, selt_ref, m_ref, l_ref, acc_ref):
    qb = NSA_QBLOCK
    nh = NSA_HEADS
    i = pl.program_id(1)
    qt = jnp.concatenate([_bf(q_ref[:, h * NSA_DH:(h + 1) * NSA_DH].astype(F32).T) for h in range(nh)], axis=1)
    selt = sel_ref[...].astype(F32).T
    selt_ref[0:8, :] = jnp.full((8, nh * qb), MASKED, F32)
    selt_ref[8:, :] = jnp.concatenate([jnp.where(selt > 0.5, 0.0, MASKED)] * nh, axis=1)
    pad_rows = 8 - 2 * SEL_PAD_TILES

    def reset():
        m_ref[...] = jnp.full_like(m_ref, M_INIT)
        l_ref[...] = jnp.zeros_like(l_ref)
        acc_ref[...] = jnp.zeros_like(acc_ref)

    def chosen(pt, ntile):
        first = pad_rows + 2 * pt
        rows = [jnp.broadcast_to(selt_ref[pl.ds(first + j, 1), :], (NSA_SEL_BLOCK, nh * qb)) for j in range(2 * ntile)]
        return jnp.concatenate(rows, axis=0)

    def tile(ref, pt, ntile=1):
        return ref[pl.ds(pl.multiple_of(pt * qb, qb), ntile * qb), :]

    reset()
    near = i + SEL_PAD_TILES - 2
    ngroup = (jnp.maximum(i - 2, 0) + FAR_GROUP - 1) // FAR_GROUP
    ones_rows = jnp.ones((16, (FAR_GROUP // 2) * qb), BF16)

    def far_body(g, carry):
        half = FAR_GROUP // 2
        pa = near - FAR_GROUP * (g + 1)
        pb = pa + half
        sa = _bf(_dot(tile(ks_ref, pa, half), qt) + chosen(pa, half))
        sb = _bf(_dot(tile(ks_ref, pb, half), qt) + chosen(pb, half))
        m0 = m_ref[...]
        ma = jnp.maximum(m0, jnp.max(sa, axis=0, keepdims=True).astype(F32))
        p_a = jnp.exp(sa - _bf(ma))
        pv_a = _dot_tn(tile(vs_ref, pa, half), p_a)
        mb = jnp.maximum(ma, jnp.max(sb, axis=0, keepdims=True).astype(F32))
        p_b = jnp.exp(sb - _bf(mb))
        pv_b = _dot_tn(tile(vs_ref, pb, half), p_b)
        al_a = jnp.exp(m0 - ma)
        al_b = jnp.exp(ma - mb)
        sum_a = jnp.sum(p_a.astype(F32), axis=0, keepdims=True)
        sum_b = jnp.sum(p_b.astype(F32), axis=0, keepdims=True)
        l_ref[...] = al_b * (al_a * l_ref[...] + sum_a) + sum_b
        acc_ref[...] = al_b * (al_a * acc_ref[...] + pv_a) + pv_b
        m_ref[...] = mb
        return carry

    lax.fori_loop(0, ngroup, far_body, 0)

    nw = NSA_WINDOW // qb
    adds = []
    for d in range(nw + 1):
        add = win_ref[d * qb:(d + 1) * qb, :]
        if d < nw:
            add = add + jnp.where(i - nw + d >= 0, 0.0, MASKED)
        adds.append(add)
    s_near = _dot(tile(ks_ref, near, 3), qt) + (chosen(near, 3) + near_ref[...])
    s_win = _dot(tile(kw_ref, i, nw + 1), qt) + jnp.concatenate(adds, axis=0)
    m_old = m_ref[...]
    m_new = jnp.maximum(m_old, jnp.max(s_near, axis=0, keepdims=True))
    p_near = jnp.exp(s_near - m_new)
    pv_near = _dot_tn(tile(vs_ref, near, 3), _bf(p_near))
    p_win = jnp.exp(s_win - jnp.max(s_win, axis=0, keepdims=True))
    pv_win = _dot_tn(tile(vw_ref, i, nw + 1), _bf(p_win))
    alpha = jnp.exp(m_old - m_new)
    l_sel = alpha * l_ref[...] + jnp.sum(p_near, axis=0, keepdims=True)
    o_sel = (alpha * acc_ref[...] + pv_near) / jnp.maximum(l_sel, 1e-30)
    o_win = pv_win / jnp.maximum(jnp.sum(p_win, axis=0, keepdims=True), 1e-30)

    gate = gate_ref[...]
    for h in range(nh):
        sl = slice(h * NSA_DH, (h + 1) * NSA_DH)
        g_sel = _sigmoid(gate[:, nh + h:nh + h + 1])
        g_win = _sigmoid(gate[:, 2 * nh + h:2 * nh + h + 1])
        o_ref[:, sl] = _bf(ocmp_ref[:, sl] + g_sel * o_sel[:, sl].T + g_win * o_win[:, sl].T)


def nsa_sel_win_attention(qn, ks, vs, kw, vw, sel, near_add, win_add, proj, ocmp, batch, seq):
    nq = seq // NSA_QBLOCK
    wide = NSA_HEADS * NSA_QBLOCK
    padded = lambda t: pl.BlockSpec((None, t.shape[1], NSA_DH), lambda b, i: (b, 0, 0))
    table = lambda t: pl.BlockSpec(t.shape, lambda b, i: (0, 0))
    rows = lambda width, cb=0: pl.BlockSpec((NSA_QBLOCK, width), lambda b, i: (b * nq + i, cb))
    return pl.pallas_call(
        _nsa_sel_win_kernel,
        grid=(batch, nq),
        in_specs=[rows(512), padded(ks), padded(vs), padded(kw), padded(vw), rows(LANE), table(near_add),
                  table(win_add), rows(LANE, P_NSA_GATE // LANE), rows(512)],
        out_specs=rows(512),
        out_shape=jax.ShapeDtypeStruct((batch * seq, 512), BF16),
        scratch_shapes=[pltpu.VMEM((8 + LANE, wide), F32), pltpu.VMEM((1, wide), F32), pltpu.VMEM((1, wide), F32),
                        pltpu.VMEM((NSA_DH, wide), F32)],
        compiler_params=_cparams("parallel", "arbitrary"),
        name="nsa_sel_win_attention",
    )(qn, ks, vs, kw, vw, sel, near_add, win_add, proj, ocmp)


def _t5_bucket_np(dist):
    n = np.maximum(dist, 0)
    max_exact = NUM_BUCKETS // 2
    nf = np.maximum(n, 1).astype(np.float64)
    large = max_exact + (np.log(nf / max_exact) / math.log(MAX_DISTANCE / max_exact)
                         * (NUM_BUCKETS - max_exact)).astype(np.int64)
    large = np.minimum(large, NUM_BUCKETS - 1)
    return np.where(n < max_exact, n, large).astype(np.int32)


def _bias_tables(rel_bias):
    qb = NSA_QBLOCK
    r = np.arange(qb)[:, None]
    l = np.arange(qb)[None, :]
    toep_idx = np.stack([_t5_bucket_np(r - l), _t5_bucket_np(qb + r - l)])
    def lookup(idx):
        onehot = (jnp.asarray(idx.reshape(-1, 1)) == jnp.arange(NUM_BUCKETS)[None, :]).astype(F32)
        return jnp.dot(onehot, rel_bias.astype(F32), precision=HI).reshape(idx.shape + (NSA_HEADS,))

    toep = jnp.transpose(lookup(toep_idx), (3, 0, 1, 2))
    per = qb // NSA_CMP_STRIDE
    band_idx = np.zeros((16, qb, LANE), np.int32)
    for im in range(16):
        base = per * im - 9
        n = base + ((np.arange(LANE) - base) % LANE)
        dist = qb * im + r - NSA_CMP_STRIDE * n[None, :] - (NSA_CMP_LEN - 1)
        band_idx[im] = _t5_bucket_np(dist)
    band = jnp.transpose(lookup(band_idx), (0, 3, 1, 2))
    cfar = rel_bias[NUM_BUCKETS - 1]
    near = jnp.transpose(toep - cfar[:, None, None, None], (1, 3, 0, 2)).reshape(2, qb, NSA_HEADS * qb)
    key = np.arange(qb)[:, None]
    qry = np.tile(np.arange(qb), NSA_HEADS)[None, :]
    diag = jnp.where(key <= qry, near[0], MASKED)
    edge = jnp.asarray(np.where(qry < key, 0.0, MASKED), F32)
    zero = jnp.zeros_like(diag)
    near_add = jnp.concatenate([zero, near[1], diag], axis=0)
    win_add = jnp.concatenate([edge, zero, zero, near[1], diag], axis=0)
    return near_add.astype(F32), win_add.astype(F32), band.astype(F32), cfar.astype(F32)


def _overlap_table(nc, seq):
    n_cmp = (seq - NSA_CMP_LEN) // NSA_CMP_STRIDE + 1
    n_sel = seq // NSA_SEL_BLOCK
    cs = np.arange(nc)[:, None] * NSA_CMP_STRIDE
    ss = np.arange(LANE)[None, :] * NSA_SEL_BLOCK
    ovl = (cs < ss + NSA_SEL_BLOCK) & (cs + NSA_CMP_LEN > ss)
    ovl &= (np.arange(nc)[:, None] < n_cmp) & (np.arange(LANE)[None, :] < n_sel)
    return jnp.asarray(ovl.astype(np.float32), dtype=BF16)


def _pad_axis(t, axis, size):
    pad = [(0, 0)] * t.ndim
    pad[axis] = (0, size - t.shape[axis])
    return jnp.pad(t, pad)


def _pack_w_in(w_in):
    nl, k, _ = w_in.shape
    src = {}
    start = 0
    names = ("gla_q", "gla_k", "gla_v", "gla_g", "gla_lr", "lru_x", "lru_g", "nsa_q", "nsa_kv", "nsa_gate", "rwkv",
             "gates")
    widths = (256, 256, 512, 512, GLA_LOWRANK, 512, 512, 512, 768, 12, 1984, N_BRANCH * D_MODEL)
    for name, wd in zip(names, widths):
        src[name] = (start, wd)
        start += wd
    def piece(name, width):
        s, wd = src[name]
        return _pad_axis(w_in[:, :, s:s + wd], 2, width)
    rs, _ = src["rwkv"]
    rw = jnp.concatenate([
        w_in[:, :, rs:rs + 1536],
        _pad_axis(w_in[:, :, rs + 1536:rs + 1632], 2, LANE),
        _pad_axis(w_in[:, :, rs + 1632:rs + 1728], 2, LANE),
        w_in[:, :, rs + 1728:rs + 1984]], axis=2)
    packed = jnp.concatenate([
        piece("gates", P_GATES_TOTAL), rw, piece("lru_x", 512), piece("lru_g", 512), piece("nsa_q", 512),
        piece("gla_v", 512), piece("gla_g", 512), piece("nsa_kv", 768), piece("gla_q", 256), piece("gla_k", 256),
        piece("gla_lr", LANE), piece("nsa_gate", LANE)], axis=2)
    assert packed.shape[2] == P_GATES_TOTAL + P_MAIN
    return packed.astype(BF16)


def _pack_rwkv_vec(t):
    return jnp.concatenate([t[:, :1536], _pad_axis(t[:, 1536:1632], 1, LANE), _pad_axis(t[:, 1632:1728], 1, LANE),
                            t[:, 1728:1984]], axis=1)[:, None, :]


def _row(t):
    return t[:, None, :]


def kernel(x, rel_bias, attn_norm, ffn_norm, w_in, gla_w_gk, gla_b_gk, gla_out_norm, lru_conv_w, lru_conv_b, lru_w_a, lru_b_a, lru_w_i, lru_b_i, lru_lambda, nsa_cmp_pos, nsa_cmp_k1, nsa_cmp_k2, nsa_cmp_v1, nsa_cmp_v2, nsa_q_norm, nsa_k_norm, rwkv_mu, rwkv_w0, rwkv_w_lora, rwkv_a0, rwkv_a_lora, rwkv_g_lora, rwkv_k_k, rwkv_k_a, rwkv_r_k, rwkv_ln_w, rwkv_ln_b, w_branch, w_out, ffn_up, ffn_conv_w, ffn_conv_b, ffn_down):
    batch, seq, d = x.shape
    depth = w_in.shape[0]
    m = batch * seq
    nc = seq // NSA_CMP_STRIDE
    assert d == D_MODEL and seq % 2048 == 0

    w_in_p = _pack_w_in(w_in)
    w_branch_b = _bf(w_branch)
    w_out_b = _bf(w_out)
    ffn_up_b = _bf(ffn_up)
    ffn_down_b = _bf(ffn_down)
    gla_w_gk_p = _pad_axis(gla_w_gk, 1, LANE)
    lru_conv_w_p = _pad_axis(lru_conv_w, 1, 8)
    ffn_conv_w_p = _pad_axis(ffn_conv_w, 1, 8)
    rwkv_w_lora_p = _bf(_pad_axis(rwkv_w_lora, 1, LANE))
    rwkv_a_lora_p = _bf(_pad_axis(rwkv_a_lora, 1, LANE))
    cmp_pos_p = jnp.broadcast_to(nsa_cmp_pos.reshape(depth, 1, NSA_CMP_LEN * NSA_DH), (depth, 8, NSA_CMP_LEN * NSA_DH))
    near_add, win_add, band, cfar = _bias_tables(rel_bias)
    ovl = _overlap_table(nc, seq)

    tm = min(1024, seq)
    xf = x.reshape(m, d)
    for l in range(depth):
        gates, proj = norm_matmul(xf, _row(attn_norm), w_in_p, l, tm, 1024)
        y_a = gla_mixer(proj, gla_w_gk_p, _row(gla_b_gk), _row(gla_out_norm), l, batch, seq)
        y_b = lru_mixer(proj, lru_conv_w_p, _row(lru_conv_b), _bf(lru_w_a), _row(lru_b_a), _bf(lru_w_i), _row(lru_b_i),
                        _row(lru_lambda), l, batch, seq, min(256, seq))
        qn, ks, vs, kw, vw = nsa_prep(proj, _row(nsa_q_norm), _row(nsa_k_norm), l, min(512, seq))
        kgrp = proj[:, P_NSA_KV:P_NSA_KV + NSA_DH].reshape(batch, nc, NSA_CMP_STRIDE * NSA_DH)
        vgrp = proj[:, P_NSA_KV + NSA_DH:P_NSA_KV + 2 * NSA_DH].reshape(batch, nc, NSA_CMP_STRIDE * NSA_DH)
        kc, vc = nsa_compress(kgrp, vgrp, cmp_pos_p, _bf(nsa_cmp_k1), _bf(nsa_cmp_k2), _bf(nsa_cmp_v1),
                              _bf(nsa_cmp_v2), _row(nsa_k_norm), l)
        ocmp, sel = nsa_cmp_attention(cfar, qn, kc, vc, band, ovl, proj, batch, seq)
        front = lambda t, tiles: jnp.pad(t.reshape(batch, seq, NSA_DH), ((0, 0), (tiles * NSA_QBLOCK, 0), (0, 0)))
        y_c = nsa_sel_win_attention(qn, front(ks, SEL_PAD_TILES), front(vs, SEL_PAD_TILES), front(kw, WIN_PAD_TILES),
                                    front(vw, WIN_PAD_TILES), sel, near_add, win_add, proj, ocmp, batch, seq)
        y_d = rwkv_mixer(proj, _pack_rwkv_vec(rwkv_mu), _row(rwkv_w0), rwkv_w_lora_p, _row(rwkv_a0), rwkv_a_lora_p,
                         _bf(rwkv_g_lora), _row(rwkv_k_k), _row(rwkv_k_a), _row(rwkv_r_k.reshape(depth, RWKV_WIDTH)),
                         _row(rwkv_ln_w), _row(rwkv_ln_b), l, batch, seq)
        merged = merge_branches((y_a, y_b, y_c, y_d), w_branch_b, gates, l, min(512, seq), 512)
        xf = matmul_residual(merged, w_out_b, xf, l, tm, 512)
        act = ffn_up_conv(xf, _row(ffn_norm), ffn_up_b, ffn_conv_w_p, _row(ffn_conv_b), l, seq, tm, 512)
        xf = matmul_residual(act, ffn_down_b, xf, l, tm, 512)
    return xf.reshape(batch, seq, d)
```

```python
import functools
import math

import numpy as np
import jax
import jax.numpy as jnp
from jax import lax
from jax.experimental import pallas as pl
from jax.experimental.pallas import tpu as pltpu

F32 = jnp.float32
BF16 = jnp.bfloat16
HI = lax.Precision.HIGHEST

LANE = 128
VMEM_LIMIT = 56 * 1024 * 1024

D_MODEL = 2048
N_BRANCH = 4
BRANCH_WIDTH = 512

GLA_HEADS, GLA_DK, GLA_DV, GLA_LOWRANK, GLA_NORMALIZER = 4, 64, 128, 16, 16.0
LRU_WIDTH, LRU_BLOCKS, LRU_CONV, LRU_C = 512, 4, 4, 8.0
NSA_HEADS, NSA_DH = 4, 128
NSA_CMP_LEN, NSA_CMP_STRIDE, NSA_SEL_BLOCK, NSA_SEL_TOPK, NSA_WINDOW, NSA_QBLOCK = 32, 16, 64, 16, 512, 128
RWKV_HEADS, RWKV_DH, RWKV_WIDTH = 8, 64, 512
RWKV_W_LORA, RWKV_A_LORA, RWKV_G_LORA = 96, 96, 256
NUM_BUCKETS, MAX_DISTANCE = 32, 128
D_FF, FFN_CONV = 5632, 3
NEG_BIG = 1e9
MASKED = -1e30
M_INIT = -1e20

P_RWKV, P_LRU_X, P_LRU_G, P_NSA_Q, P_GLA_V, P_GLA_G = 0, 2048, 2560, 3072, 3584, 4096
P_NSA_KV, P_GLA_Q, P_GLA_K, P_GLA_LR, P_NSA_GATE, P_MAIN = 4608, 5376, 5632, 5888, 6016, 6144
P_GATES_TOTAL = 8192
RW_R, RW_K, RW_V, RW_XW, RW_XA, RW_XG, RW_TOTAL = 0, 512, 1024, 1536, 1664, 1792, 2048

CHUNK = 64


def _cparams(*sem):
    return pltpu.CompilerParams(dimension_semantics=sem, vmem_limit_bytes=VMEM_LIMIT)


def _dot(a, b, prec=None):
    return jnp.dot(a, b, preferred_element_type=F32, precision=prec)


def _dot_nt(a, b, prec=None):
    return lax.dot_general(a, b, (((1,), (1,)), ((), ())), preferred_element_type=F32, precision=prec)


def _dot_tn(a, b, prec=None):
    return lax.dot_general(a, b, (((0,), (0,)), ((), ())), preferred_element_type=F32, precision=prec)


def _bf(x):
    return x.astype(BF16)


def _sigmoid(x):
    return 1.0 / (1.0 + jnp.exp(-x))


def _log_sigmoid(x):
    return jnp.minimum(x, 0.0) - jnp.log(1.0 + jnp.exp(-jnp.abs(x)))


def _gelu_tanh(x):
    return 0.5 * x * (1.0 + jnp.tanh(math.sqrt(2.0 / math.pi) * (x + 0.044715 * (x * x * x))))


def _iota(shape, dim):
    return lax.broadcasted_iota(jnp.int32, shape, dim)


def _idiv(x, d):
    assert d & (d - 1) == 0
    return jnp.right_shift(x, d.bit_length() - 1)


def _shift_rows(x, s, top):
    xr = pltpu.roll(x, s, 0)
    tr = pltpu.roll(top, s, 0)
    row = _iota((8, x.shape[1]), 0)
    head = jnp.where(row < s, tr, xr[:8])
    return jnp.concatenate([head, xr[8:]], axis=0)


def _norm_mm_kernel(x_ref, g_ref, w_ref, og_ref, om_ref, xn_ref, *, gate_tiles):
    j = pl.program_id(1)

    @pl.when(j == 0)
    def _():
        x = x_ref[...]
        ms = jnp.mean(x * x, axis=-1, keepdims=True)
        xn_ref[...] = _bf(x * lax.rsqrt(ms + 1e-6) * g_ref[...])

    @pl.when(j < gate_tiles)
    def _():
        og_ref[...] = _bf(_dot(xn_ref[...], w_ref[...]))

    @pl.when(j >= gate_tiles)
    def _():
        om_ref[...] = _dot(xn_ref[...], w_ref[...])


def norm_matmul(x, gain, w, layer, tm, tn):
    m, k = x.shape
    n = w.shape[2]
    gate_tiles = P_GATES_TOTAL // tn
    kern = functools.partial(_norm_mm_kernel, gate_tiles=gate_tiles)
    return pl.pallas_call(
        kern,
        grid=(m // tm, n // tn),
        in_specs=[pl.BlockSpec((tm, k), lambda i, j: (i, 0)),
                  pl.BlockSpec((None, 1, k), lambda i, j: (layer, 0, 0)),
                  pl.BlockSpec((None, k, tn), lambda i, j: (layer, 0, j))],
        out_specs=[pl.BlockSpec((tm, tn), lambda i, j: (i, jnp.minimum(j, gate_tiles - 1))),
                   pl.BlockSpec((tm, tn), lambda i, j: (i, jnp.maximum(j - gate_tiles, 0)))],
        out_shape=[jax.ShapeDtypeStruct((m, P_GATES_TOTAL), BF16), jax.ShapeDtypeStruct((m, n - P_GATES_TOTAL), F32)],
        scratch_shapes=[pltpu.VMEM((tm, k), BF16)],
        compiler_params=_cparams("parallel", "arbitrary"),
        name="norm_matmul",
    )(x, gain, w)


def _mm_res_kernel(a_ref, w_ref, r_ref, o_ref):
    o_ref[...] = r_ref[...] + _dot(a_ref[...], w_ref[...])


def matmul_residual(a, w, res, layer, tm, tn):
    m, k = a.shape
    n = w.shape[2]
    return pl.pallas_call(
        _mm_res_kernel,
        grid=(m // tm, n // tn),
        in_specs=[pl.BlockSpec((tm, k), lambda i, j: (i, 0)),
                  pl.BlockSpec((None, k, tn), lambda i, j: (layer, 0, j)),
                  pl.BlockSpec((tm, tn), lambda i, j: (i, j))],
        out_specs=pl.BlockSpec((tm, tn), lambda i, j: (i, j)),
        out_shape=jax.ShapeDtypeStruct((m, n), F32),
        compiler_params=_cparams("parallel", "arbitrary"),
        name="matmul_residual",
    )(a, w, res)


def _merge_kernel(ya_ref, yb_ref, yc_ref, yd_ref, wb_ref, g0_ref, g1_ref, g2_ref, g3_ref, o_ref):
    acc = None
    for n, (y_ref, g_ref) in enumerate(((ya_ref, g0_ref), (yb_ref, g1_ref), (yc_ref, g2_ref), (yd_ref, g3_ref))):
        t = _sigmoid(g_ref[...].astype(F32)) * _dot(y_ref[...], wb_ref[n])
        acc = t if acc is None else acc + t
    o_ref[...] = _bf(acc)


def merge_branches(ys, w_branch, gates, layer, tm, tn):
    m = ys[0].shape[0]
    nj = D_MODEL // tn
    y_spec = pl.BlockSpec((tm, BRANCH_WIDTH), lambda i, j: (i, 0))
    gate_specs = [pl.BlockSpec((tm, tn), functools.partial(lambda i, j, n: (i, (n * D_MODEL) // tn + j), n=n))
                  for n in range(N_BRANCH)]
    return pl.pallas_call(
        _merge_kernel,
        grid=(m // tm, nj),
        in_specs=[y_spec] * 4 + [pl.BlockSpec((None, N_BRANCH, BRANCH_WIDTH, tn), lambda i, j: (layer, 0, 0, j))]
        + gate_specs,
        out_specs=pl.BlockSpec((tm, tn), lambda i, j: (i, j)),
        out_shape=jax.ShapeDtypeStruct((m, D_MODEL), BF16),
        compiler_params=_cparams("parallel", "arbitrary"),
        name="merge_branches",
    )(*ys, w_branch, gates, gates, gates, gates)


def _ffn_up_kernel(x_ref, g_ref, wg_ref, wv_ref, cw_g_ref, cw_v_ref, cb_g_ref, cb_v_ref, o_ref,
                   xn_ref, carry_ref, *, tiles_per_seq):
    i, j = pl.program_id(0), pl.program_id(1)

    @pl.when(j == 0)
    def _():
        x = x_ref[...]
        ms = jnp.mean(x * x, axis=-1, keepdims=True)
        xn_ref[...] = _bf(x * lax.rsqrt(ms + 1e-6) * g_ref[...])

    first = (i % tiles_per_seq) == 0

    @pl.when(first)
    def _():
        carry_ref[j] = jnp.zeros(carry_ref.shape[1:], F32)

    xn = xn_ref[...]
    outs = []
    for half, (w_ref, cw_ref, cb_ref) in enumerate(((wg_ref, cw_g_ref, cb_g_ref), (wv_ref, cw_v_ref, cb_v_ref))):
        u = _dot(xn, w_ref[...])
        top = carry_ref[j, half]
        carry_ref[j, half] = u[u.shape[0] - 8:]
        cw = cw_ref[...]
        outs.append(cw[0:1] * _shift_rows(u, 2, top) + cw[1:2] * _shift_rows(u, 1, top) + cw[2:3] * u + cb_ref[...])
    gate, val = outs
    o_ref[...] = _bf(gate * _sigmoid(gate) * val)


def ffn_up_conv(x, gain, w_up, conv_w, conv_b, layer, seq, tm, tn):
    m, k = x.shape
    nj = D_FF // tn
    kern = functools.partial(_ffn_up_kernel, tiles_per_seq=seq // tm)
    return pl.pallas_call(
        kern,
        grid=(m // tm, nj),
        in_specs=[pl.BlockSpec((tm, k), lambda i, j: (i, 0)),
                  pl.BlockSpec((None, 1, k), lambda i, j: (layer, 0, 0)),
                  pl.BlockSpec((None, k, tn), lambda i, j: (layer, 0, j)),
                  pl.BlockSpec((None, k, tn), lambda i, j: (layer, 0, nj + j)),
                  pl.BlockSpec((None, 8, tn), lambda i, j: (layer, 0, j)),
                  pl.BlockSpec((None, 8, tn), lambda i, j: (layer, 0, nj + j)),
                  pl.BlockSpec((None, 1, tn), lambda i, j: (layer, 0, j)),
                  pl.BlockSpec((None, 1, tn), lambda i, j: (layer, 0, nj + j))],
        out_specs=pl.BlockSpec((tm, tn), lambda i, j: (i, j)),
        out_shape=jax.ShapeDtypeStruct((m, D_FF), BF16),
        scratch_shapes=[pltpu.VMEM((tm, k), BF16), pltpu.VMEM((nj, 2, 8, tn), F32)],
        compiler_params=_cparams("arbitrary", "arbitrary"),
        name="ffn_up",
    )(x, gain, w_up, w_up, conv_w, conv_w, conv_b, conv_b)


def _gla_kernel(q_ref, k_ref, v_ref, g_ref, lr_ref, wgk_ref, bgk_ref, gain_ref, o_ref, st_ref):
    c = CHUNK

    nb = q_ref.shape[0]
    npair = GLA_HEADS // 2

    @pl.when(pl.program_id(0) == 0)
    def _():
        st_ref[...] = jnp.zeros_like(st_ref)

    row = _iota((c, c), 0)
    col = _iota((c, c), 1)
    tri_incl = (col <= row).astype(F32)
    lane = _iota((1, LANE), 1)
    bd = _idiv(_iota((2 * GLA_DV, LANE), 0), GLA_DV) == _idiv(_iota((2 * GLA_DV, LANE), 1), GLA_DK)
    sub = 16
    nsub = c // sub
    scores = {}
    o_inter = {}
    vbs = []
    for bi in range(nb):
        log_a = _log_sigmoid(_dot(lr_ref[bi], wgk_ref[...], HI) + bgk_ref[...]) * (1.0 / GLA_NORMALIZER)
        b = _dot(tri_incl, log_a, HI)
        vb = _bf(v_ref[bi])
        vbs.append(vb)
        for p in range(npair):
            sl = slice(p * LANE, (p + 1) * LANE)
            qp = q_ref[bi, :, sl] * (GLA_DK ** -0.5)
            kp = k_ref[bi, :, sl]
            bp = b[:, sl]
            b_last = bp[c - 1:c]
            st = st_ref[bi * npair + p]
            o_inter[bi, p] = _dot_nt(_bf(qp * jnp.exp(bp)), _bf(st))
            upd = _dot_tn(vb[:, 2 * p * GLA_DV:(2 * p + 2) * GLA_DV], _bf(kp * jnp.exp(b_last - bp)))
            st_ref[bi * npair + p] = st * jnp.exp(b_last) + jnp.where(bd, upd, 0.0)
            for i in range(nsub):
                rows = slice(i * sub, (i + 1) * sub)
                nrow = (i + 1) * sub
                bref = bp[i * sub - 1:i * sub] if i > 0 else jnp.zeros((1, LANE), F32)
                qi = qp[rows] * jnp.exp(bp[rows] - bref)
                ki = _bf(kp[:nrow] * jnp.exp(jnp.minimum(bref - bp[:nrow], 80.0)))
                causal = _iota((sub, nrow), 1) <= (_iota((sub, nrow), 0) + i * sub)
                for hh in range(2):
                    head_lanes = _idiv(lane, GLA_DK) == hh
                    s = _dot_nt(_bf(jnp.where(head_lanes, qi, 0.0)), ki)
                    scores[bi, 2 * p + hh, i] = _bf(jnp.where(causal, s, 0.0))
    gain = gain_ref[...]
    for bi in range(nb):
        for h in range(GLA_HEADS):
            hs = slice(h * GLA_DV, (h + 1) * GLA_DV)
            intra = [_dot(scores[bi, h, i], vbs[bi][:(i + 1) * sub, hs]) for i in range(nsub)]
            hh = h % 2
            o = o_inter[bi, h // 2][:, hh * GLA_DV:(hh + 1) * GLA_DV] + jnp.concatenate(intra, axis=0)
            y = o * lax.rsqrt(jnp.mean(o * o, axis=-1, keepdims=True) + 1e-6) * gain
            g = g_ref[bi, :, hs]
            o_ref[bi, :, hs] = _bf(y * (g * _sigmoid(g)))


def gla_mixer(proj, w_gk, b_gk, out_gain, layer, batch, seq):
    nt = seq // CHUNK
    def col(off, width):
        return pl.BlockSpec((batch, CHUNK, width), lambda t: (0, t, off // width))
    proj3 = proj.reshape(batch, seq, proj.shape[1])
    out = pl.pallas_call(
        _gla_kernel,
        grid=(nt,),
        in_specs=[col(P_GLA_Q, 256), col(P_GLA_K, 256), col(P_GLA_V, 512), col(P_GLA_G, 512), col(P_GLA_LR, 128),
                  pl.BlockSpec((None, LANE, 256), lambda t: (layer, 0, 0)),
                  pl.BlockSpec((None, 1, 256), lambda t: (layer, 0, 0)),
                  pl.BlockSpec((None, 1, GLA_DV), lambda t: (layer, 0, 0))],
        out_specs=pl.BlockSpec((batch, CHUNK, 512), lambda t: (0, t, 0)),
        out_shape=jax.ShapeDtypeStruct((batch, seq, 512), BF16),
        scratch_shapes=[pltpu.VMEM((batch * (GLA_HEADS // 2), 2 * GLA_DV, LANE), F32)],
        compiler_params=_cparams("arbitrary"),
        name="gla_mixer",
    )(proj3, proj3, proj3, proj3, proj3, w_gk, b_gk, out_gain)
    return out.reshape(batch * seq, 512)


def _lru_kernel(x_ref, gb_ref, cw_ref, cb_ref, wa_ref, ba_ref, wi_ref, bi_ref, lam_ref, o_ref, xc_ref, h_ref):
    t = x_ref.shape[0]

    @pl.when(pl.program_id(1) == 0)
    def _():
        xc_ref[...] = jnp.zeros_like(xc_ref)
        h_ref[...] = jnp.zeros_like(h_ref)

    x = x_ref[...]
    top = xc_ref[...]
    xc_ref[...] = x[t - 8:]
    cw = cw_ref[...]
    xc = (cw[0:1] * _shift_rows(x, 3, top) + cw[1:2] * _shift_rows(x, 2, top) + cw[2:3] * _shift_rows(x, 1, top)
          + cw[3:4] * x + cb_ref[...])
    xcb = _bf(xc)
    ra, ri = [], []
    for n in range(LRU_BLOCKS):
        blk = xcb[:, n * LANE:(n + 1) * LANE]
        ra.append(_dot(blk, wa_ref[n]))
        ri.append(_dot(blk, wi_ref[n]))
    r = _sigmoid(jnp.concatenate(ra, axis=1) + ba_ref[...])
    gi = _sigmoid(jnp.concatenate(ri, axis=1) + bi_ref[...])
    log_a = LRU_C * r * _log_sigmoid(lam_ref[...])
    a = jnp.exp(log_a)
    u = jnp.sqrt(-jnp.tanh(log_a) * (a * a + 1.0)) * (gi * xc)
    row = _iota((t, LRU_WIDTH), 0)
    k = 1
    while k < t:
        a_sh = jnp.where(row < k, 1.0, pltpu.roll(a, k, 0))
        u_sh = jnp.where(row < k, 0.0, pltpu.roll(u, k, 0))
        u = u + a * u_sh
        a = a * a_sh
        k *= 2
    h = a * h_ref[0:1] + u
    h_ref[...] = jnp.broadcast_to(h[t - 1:t], h_ref.shape)
    o_ref[...] = _bf(h * _gelu_tanh(gb_ref[...]))


def lru_mixer(proj, conv_w, conv_b, w_a, b_a, w_i, b_i, lam, layer, batch, seq, tt):
    nt = seq // tt
    vec = pl.BlockSpec((None, 1, LRU_WIDTH), lambda b, t: (layer, 0, 0))
    wblk = pl.BlockSpec((None, LRU_BLOCKS, LANE, LANE), lambda b, t: (layer, 0, 0, 0))
    return pl.pallas_call(
        _lru_kernel,
        grid=(batch, nt),
        in_specs=[pl.BlockSpec((tt, 512), lambda b, t: (b * nt + t, P_LRU_X // 512)),
                  pl.BlockSpec((tt, 512), lambda b, t: (b * nt + t, P_LRU_G // 512)),
                  pl.BlockSpec((None, 8, LRU_WIDTH), lambda b, t: (layer, 0, 0)), vec, wblk, vec, wblk, vec, vec],
        out_specs=pl.BlockSpec((tt, 512), lambda b, t: (b * nt + t, 0)),
        out_shape=jax.ShapeDtypeStruct((batch * seq, 512), BF16),
        scratch_shapes=[pltpu.VMEM((8, LRU_WIDTH), F32), pltpu.VMEM((8, LRU_WIDTH), F32)],
        compiler_params=_cparams("parallel", "arbitrary"),
        name="lru_mixer",
    )(proj, proj, conv_w, conv_b, w_a, b_a, w_i, b_i, lam)


def _rwkv_kernel(f_ref, mu_ref, w0_ref, wl_ref, a0_ref, al_ref, gl_ref, kk_ref, ka_ref, rk_ref, lnw_ref, lnb_ref,
                 o_ref, st_ref, prev_ref):
    c = CHUNK
    n = RWKV_DH

    nb = f_ref.shape[0]

    @pl.when(pl.program_id(0) == 0)
    def _():
        st_ref[...] = jnp.zeros_like(st_ref)
        prev_ref[...] = jnp.zeros_like(prev_ref)

    feats = [f_ref[b] for b in range(nb)]
    prev = jnp.concatenate([_shift_rows(feats[b], 1, prev_ref[b]) for b in range(nb)], axis=0)
    for b in range(nb):
        prev_ref[b] = feats[b][c - 8:]
    feat = jnp.concatenate(feats, axis=0)
    xm = feat + (prev - feat) * mu_ref[...]
    r = xm[:, RW_R:RW_R + 512]
    k = xm[:, RW_K:RW_K + 512]
    v = xm[:, RW_V:RW_V + 512]
    log_w = -math.exp(-0.5) * _sigmoid(w0_ref[...] + _dot(_bf(jnp.tanh(xm[:, RW_XW:RW_XW + LANE])), wl_ref[...]))
    a = _sigmoid(a0_ref[...] + _dot(_bf(xm[:, RW_XA:RW_XA + LANE]), al_ref[...]))
    g = _dot(_bf(_sigmoid(xm[:, RW_XG:RW_XG + 256])), gl_ref[...])

    assert c == n and 2 * n == LANE
    head0 = _iota((1, LANE), 1) < n
    same_head = _idiv(_iota((LANE, LANE), 0), n) == _idiv(_iota((LANE, LANE), 1), n)
    head_ones = _bf(same_head.astype(F32))
    row = _iota((c, LANE), 0)
    col = jnp.bitwise_and(_iota((c, LANE), 1), n - 1)
    tri_incl = col <= row
    tri_strict = col < row
    rr_ = _iota((nb * c, nb * c), 0)
    cc_ = _iota((nb * c, nb * c), 1)
    tri_sq = _bf(((cc_ <= rr_) & (_idiv(cc_, c) == _idiv(rr_, c))).astype(F32))

    def stack(t):
        return jnp.concatenate([jnp.where(head0, t, 0.0), jnp.where(head0, 0.0, t)], axis=0)

    def seg_sum(t):
        hi = _bf(t)
        return _dot(hi, head_ones) + _dot(_bf(t - hi.astype(F32)), head_ones)

    lw_hi = _bf(log_w)
    lw_r = log_w - lw_hi.astype(F32)
    lw_mid = _bf(lw_r)
    cum_all = _dot(tri_sq, lw_hi) + _dot(tri_sq, lw_mid) + _dot(tri_sq, _bf(lw_r - lw_mid.astype(F32)))

    pairs = range(RWKV_HEADS // 2)
    sls = [slice(p * LANE, (p + 1) * LANE) for p in pairs]
    kk = [k[:, sl] * kk_ref[:, sl] for sl in sls]
    ss = [seg_sum(t * t) for t in kk]
    kk = [t / jnp.maximum(jnp.sqrt(s), 1e-12) for t, s in zip(kk, ss)]
    k2 = [k[:, sl] * (1.0 + (a[:, sl] - 1.0) * ka_ref[:, sl]) for sl in sls]
    e_neg = [jnp.exp(-cum_all[:, sl]) for sl in sls]
    rt_all = [_bf(r[:, sl] * jnp.exp(cum_all[:, sl])) for sl in sls]
    kt_all = [_bf(kk[p] * jnp.exp(cum_all[:, sls[p]] - log_w[:, sls[p]])) for p in pairs]
    kb_all = [k2[p] * e_neg[p] for p in pairs]
    bb_all = [kk[p] * a[:, sls[p]] * e_neg[p] for p in pairs]
    chains = [(b, p) for b in range(nb) for p in pairs]
    nch = range(len(chains))
    rows = [slice(b * c, (b + 1) * c) for b, _ in chains]
    e_last = [jnp.exp(cum_all[b * c + c - 1:(b + 1) * c, sls[p]]) for b, p in chains]
    rt = [rt_all[p][rows[i]] for i, (_, p) in enumerate(chains)]
    kt = [kt_all[p][rows[i]] for i, (_, p) in enumerate(chains)]
    kb = [kb_all[p][rows[i]] for i, (_, p) in enumerate(chains)]
    bb = [bb_all[p][rows[i]] for i, (_, p) in enumerate(chains)]
    vv = [v[rows[i], sls[p]] for i, (_, p) in enumerate(chains)]
    kb2 = [_bf(stack(t)) for t in kb]
    bb2 = [_bf(stack(t)) for t in bb]
    vstk = [_bf(stack(t)) for t in vv]
    st = [st_ref[i] for i in nch]
    stb = [_bf(t) for t in st]
    a_mat = [jnp.where(tri_strict, _dot_nt(kt[i], bb2[i]), 0.0) for i in nch]
    b_mat = [jnp.where(tri_strict, _dot_nt(kt[i], kb2[i]), 0.0) for i in nch]
    rr_mat = [jnp.where(tri_incl, _dot_nt(rt[i], kb2[i]), 0.0) for i in nch]
    rb_mat = [jnp.where(tri_incl, _dot_nt(rt[i], bb2[i]), 0.0) for i in nch]
    rhs = [_dot_nt(kt[i], stb[i]) + _dot(_bf(b_mat[i]), vstk[i]) for i in nch]
    y0 = [_dot_nt(rt[i], stb[i]) + _dot(_bf(rr_mat[i]), vstk[i]) for i in nch]
    nmat = [-t for t in a_mat]
    tm1 = list(nmat)
    pw = 1
    while 2 * pw < c:
        nmat = [_dot(_bf(t), _bf(stack(t))) for t in nmat]
        tm1 = [tm1[i] + nmat[i] + _dot(_bf(tm1[i]), _bf(stack(nmat[i]))) for i in nch]
        pw *= 2
    u = [rhs[i] + _dot(_bf(tm1[i]), _bf(stack(rhs[i]))) for i in nch]
    y = [y0[i] - _dot(_bf(rb_mat[i]), _bf(stack(u[i]))) for i in nch]
    upd = [_dot_tn(_bf(jnp.concatenate([vv[i], -u[i]], axis=0)),
                   _bf(jnp.concatenate([kb[i], bb[i]], axis=0) * e_last[i])) for i in nch]
    for i in nch:
        st_ref[i] = st[i] * e_last[i] + jnp.where(same_head, upd[i], 0.0)
    y_all = [jnp.concatenate([y[b * len(pairs) + p] for b in range(nb)], axis=0) for p in pairs]
    yc = [y_all[p] - seg_sum(y_all[p]) * (1.0 / n) for p in pairs]
    var = [seg_sum(t * t) * (1.0 / n) for t in yc]
    bonus = [seg_sum(r[:, sls[p]] * k2[p] * rk_ref[:, sls[p]]) for p in pairs]
    out = [(yc[p] * lax.rsqrt(var[p] + 64e-5) * lnw_ref[:, sls[p]] + lnb_ref[:, sls[p]] + bonus[p] * v[:, sls[p]])
           for p in pairs]
    out = _bf(jnp.concatenate(out, axis=1) * g)
    for b in range(nb):
        o_ref[b] = out[b * c:(b + 1) * c]


def rwkv_mixer(proj, mu, w0, w_lora, a0, a_lora, g_lora, k_k, k_a, r_k, ln_w, ln_b, layer, batch, seq):
    nt = seq // CHUNK
    vec = pl.BlockSpec((None, 1, RWKV_WIDTH), lambda t: (layer, 0, 0))
    out = pl.pallas_call(
        _rwkv_kernel,
        grid=(nt,),
        in_specs=[pl.BlockSpec((batch, CHUNK, RW_TOTAL), lambda t: (0, t, P_RWKV // RW_TOTAL)),
                  pl.BlockSpec((None, 1, RW_TOTAL), lambda t: (layer, 0, 0)),
                  vec, pl.BlockSpec((None, LANE, RWKV_WIDTH), lambda t: (layer, 0, 0)),
                  vec, pl.BlockSpec((None, LANE, RWKV_WIDTH), lambda t: (layer, 0, 0)),
                  pl.BlockSpec((None, RWKV_G_LORA, RWKV_WIDTH), lambda t: (layer, 0, 0)),
                  vec, vec, vec, vec, vec],
        out_specs=pl.BlockSpec((batch, CHUNK, 512), lambda t: (0, t, 0)),
        out_shape=jax.ShapeDtypeStruct((batch, seq, 512), BF16),
        scratch_shapes=[pltpu.VMEM((batch * (RWKV_HEADS // 2), LANE, LANE), F32),
                        pltpu.VMEM((batch, 8, RW_TOTAL), F32)],
        compiler_params=_cparams("arbitrary"),
        name="rwkv_mixer",
    )(proj.reshape(batch, seq, proj.shape[1]), mu, w0, w_lora, a0, a_lora, g_lora, k_k, k_a, r_k, ln_w, ln_b)
    return out.reshape(batch * seq, 512)


def _head_rms(x, gain):
    return x * lax.rsqrt(jnp.mean(x * x, axis=-1, keepdims=True) + 1e-6) * gain


def _nsa_prep_kernel(q_ref, kv_ref, qg_ref, kg_ref, qn_ref, ks_ref, vs_ref, kw_ref, vw_ref):
    qg = qg_ref[...] * (NSA_DH ** -0.5)
    for h in range(NSA_HEADS):
        sl = slice(h * NSA_DH, (h + 1) * NSA_DH)
        qn_ref[:, sl] = _bf(_head_rms(q_ref[:, sl], qg))
    kg = kg_ref[...]
    ks_ref[...] = _bf(_head_rms(kv_ref[:, 2 * NSA_DH:3 * NSA_DH], kg))
    vs_ref[...] = _bf(kv_ref[:, 3 * NSA_DH:4 * NSA_DH])
    kw_ref[...] = _bf(_head_rms(kv_ref[:, 4 * NSA_DH:5 * NSA_DH], kg))
    vw_ref[...] = _bf(kv_ref[:, 5 * NSA_DH:6 * NSA_DH])


def nsa_prep(proj, q_gain, k_gain, layer, tt):
    m = proj.shape[0]
    gain = pl.BlockSpec((None, 1, NSA_DH), lambda i: (layer, 0, 0))
    kv_out = pl.BlockSpec((tt, NSA_DH), lambda i: (i, 0))
    kv_shape = jax.ShapeDtypeStruct((m, NSA_DH), BF16)
    return pl.pallas_call(
        _nsa_prep_kernel,
        grid=(m // tt,),
        in_specs=[pl.BlockSpec((tt, 512), lambda i: (i, P_NSA_Q // 512)),
                  pl.BlockSpec((tt, 768), lambda i: (i, P_NSA_KV // 768)), gain, gain],
        out_specs=[pl.BlockSpec((tt, 512), lambda i: (i, 0)), kv_out, kv_out, kv_out, kv_out],
        out_shape=[jax.ShapeDtypeStruct((m, 512), BF16), kv_shape, kv_shape, kv_shape, kv_shape],
        compiler_params=_cparams("parallel"),
        name="nsa_prep",
    )(proj, proj, q_gain, k_gain)


def _nsa_compress_kernel(kg_ref, vg_ref, pos_ref, k1_ref, k2_ref, v1_ref, v2_ref, gain_ref, kc_ref, vc_ref):
    nc = kg_ref.shape[0]
    half = NSA_CMP_STRIDE * NSA_DH
    pos = _bf(pos_ref[...])

    def compress(g_ref, w1_ref, w2_ref):
        grp = _bf(g_ref[...])
        first = _dot(grp, w1_ref[:half])
        second = _dot(grp, w1_ref[half:])
        const = _dot(pos, w1_ref[...])[0:1]
        hid = first + pltpu.roll(second, nc - 1, 0) + const
        return _dot(_bf(_gelu_tanh(hid)), w2_ref[...])

    kc_ref[...] = _bf(_head_rms(compress(kg_ref, k1_ref, k2_ref), gain_ref[...]))
    vc_ref[...] = _bf(compress(vg_ref, v1_ref, v2_ref))


def nsa_compress(kgrp, vgrp, pos, k1, k2, v1, v2, k_gain, layer):
    batch, nc, width = kgrp.shape
    grp = pl.BlockSpec((None, nc, width), lambda b: (b, 0, 0))
    w1 = pl.BlockSpec((None, 2 * width, NSA_DH), lambda b: (layer, 0, 0))
    w2 = pl.BlockSpec((None, NSA_DH, NSA_DH), lambda b: (layer, 0, 0))
    out = pl.BlockSpec((None, nc, NSA_DH), lambda b: (b, 0, 0))
    shape = jax.ShapeDtypeStruct((batch, nc, NSA_DH), BF16)
    return pl.pallas_call(
        _nsa_compress_kernel,
        grid=(batch,),
        in_specs=[grp, grp, pl.BlockSpec((None, 8, 2 * width), lambda b: (layer, 0, 0)), w1, w2, w1, w2,
                  pl.BlockSpec((None, 1, NSA_DH), lambda b: (layer, 0, 0))],
        out_specs=[out, out],
        out_shape=[shape, shape],
        compiler_params=_cparams("parallel"),
        name="nsa_compress",
    )(kgrp, vgrp, pos, k1, k2, v1, v2, k_gain)


CMP_GROUP = 8


def _nsa_cmp_kernel(cfar_ref, q_ref, kc_ref, vc_ref, *rest):
    band_refs = rest[:CMP_GROUP]
    ovl_ref, gate_ref, o_ref, sel_ref = rest[CMP_GROUP:]
    qb = NSA_QBLOCK
    nc = kc_ref.shape[0]
    kc = kc_ref[...]
    vc = vc_ref[...]
    ovl = ovl_ref[...]
    r = _iota((qb, nc), 0)
    ncol = _iota((qb, nc), 1)
    blk = _iota((qb, LANE), 1)
    works = []
    for g in range(CMP_GROUP):
        i = pl.program_id(1) * CMP_GROUP + g
        rows = slice(g * qb, (g + 1) * qb)
        dist = qb * i + r - NSA_CMP_STRIDE * ncol - (NSA_CMP_LEN - 1)
        visible = dist >= 0
        lo = (qb // NSA_CMP_STRIDE) * i - 9
        in_band = (ncol >= lo) & (ncol <= lo + 15)
        gate = gate_ref[rows, :]
        p_sum = jnp.zeros((qb, nc), F32)
        for h in range(NSA_HEADS):
            sl = slice(h * NSA_DH, (h + 1) * NSA_DH)
            band = jnp.concatenate([band_refs[g][h]] * (nc // LANE), axis=1)
            logit = _dot_nt(q_ref[rows, sl], kc) + jnp.where(in_band, band, cfar_ref[h])
            logit = jnp.where(visible, logit, MASKED)
            mx = jnp.max(logit, axis=-1, keepdims=True)
            p = jnp.where(visible, jnp.exp(logit - mx), 0.0)
            p = p / jnp.maximum(jnp.sum(p, axis=-1, keepdims=True), 1e-30)
            p_sum = p_sum + p
            o_ref[rows, sl] = _sigmoid(gate[:, h:h + 1]) * _dot(_bf(p), vc)
        p_hi = _bf(p_sum)
        score = _dot(p_hi, ovl) + _dot(_bf(p_sum - p_hi.astype(F32)), ovl)
        pos = qb * i + _iota((qb, LANE), 0)
        cur = _idiv(pos, NSA_SEL_BLOCK)
        forced = (blk == 0) | (blk == cur) | (blk == cur - 1)
        works.append(jnp.where(forced, NEG_BIG, jnp.where(blk * NSA_SEL_BLOCK <= pos, score, -NEG_BIG)))
    work = jnp.concatenate(works, axis=0)
    sel = jnp.zeros(work.shape, F32)
    blk_f = _iota(work.shape, 1).astype(F32)
    for _ in range(NSA_SEL_TOPK):
        mx = jnp.max(work, axis=-1, keepdims=True)
        first = jnp.min(jnp.where(work == mx, blk_f, float(LANE)), axis=-1, keepdims=True)
        pick = blk_f == first
        sel = jnp.where(pick, 1.0, sel)
        work = jnp.where(pick, -jnp.inf, work)
    sel_ref[...] = _bf(sel)


def nsa_cmp_attention(cfar, qn, kc, vc, band, ovl, proj, batch, seq):
    rows = CMP_GROUP * NSA_QBLOCK
    ns = seq // rows
    nc = kc.shape[1]
    full = pl.BlockSpec((None, nc, NSA_DH), lambda b, i: (b, 0, 0))
    band_specs = [pl.BlockSpec((None, NSA_HEADS, NSA_QBLOCK, LANE),
                               functools.partial(lambda b, i, g: ((i * CMP_GROUP + g) % 16, 0, 0, 0), g=g))
                  for g in range(CMP_GROUP)]
    return pl.pallas_call(
        _nsa_cmp_kernel,
        grid=(batch, ns),
        in_specs=[pl.BlockSpec(memory_space=pltpu.SMEM),
                  pl.BlockSpec((rows, 512), lambda b, i: (b * ns + i, 0)), full, full] + band_specs
        + [pl.BlockSpec((nc, LANE), lambda b, i: (0, 0)),
           pl.BlockSpec((rows, LANE), lambda b, i: (b * ns + i, P_NSA_GATE // LANE))],
        out_specs=[pl.BlockSpec((rows, 512), lambda b, i: (b * ns + i, 0)),
                   pl.BlockSpec((rows, LANE), lambda b, i: (b * ns + i, 0))],
        out_shape=[jax.ShapeDtypeStruct((batch * seq, 512), F32), jax.ShapeDtypeStruct((batch * seq, LANE), BF16)],
        compiler_params=_cparams("parallel", "arbitrary"),
        name="nsa_cmp_attention",
    )(cfar, qn, kc, vc, *([band] * CMP_GROUP), ovl, proj)


FAR_GROUP = 8
FAR_CHUNK = 2
SEL_PAD_TILES = FAR_GROUP
SEL_PAD_ROWS = 2 * SEL_PAD_TILES
assert SEL_PAD_ROWS % 8 == 0 and NSA_QBLOCK == 2 * NSA_SEL_BLOCK
WIN_PAD_TILES = NSA_WINDOW // NSA_QBLOCK


def _nsa_sel_win_kernel(q_ref, ks_ref, vs_ref, kw_ref, vw_ref, sel_ref, near_ref, win_ref, gate_ref, ocmp_ref,
                        o_ref, selt_ref, m_ref, l_ref, acc_ref):
    qb = NSA_QBLOCK
    nh = NSA_HEADS
    i = pl.program_id(1)
    qt = jnp.concatenate([_bf(q_ref[:, h * NSA_DH:(h + 1) * NSA_DH].astype(F32).T) for h in range(nh)], axis=1)
    selt = sel_ref[...].astype(F32).T
    selt_ref[0:SEL_PAD_ROWS, :] = jnp.full((SEL_PAD_ROWS, nh * qb), MASKED, F32)
    selt_ref[SEL_PAD_ROWS:, :] = jnp.concatenate([jnp.where(selt > 0.5, 0.0, MASKED)] * nh, axis=1)

    def reset():
        m_ref[...] = jnp.full_like(m_ref, M_INIT)
        l_ref[...] = jnp.zeros_like(l_ref)
        acc_ref[...] = jnp.zeros_like(acc_ref)

    def chosen(pt, ntile):
        first = 2 * pt
        rows = [jnp.broadcast_to(selt_ref[pl.ds(first + j, 1), :], (NSA_SEL_BLOCK, nh * qb)) for j in range(2 * ntile)]
        return jnp.concatenate(rows, axis=0)

    def tile(ref, pt, ntile=1):
        return ref[pl.ds(pl.multiple_of(pt * qb, qb), ntile * qb), :]

    reset()
    near = i + SEL_PAD_TILES - 2
    ngroup = (jnp.maximum(i - 2, 0) + FAR_GROUP - 1) // FAR_GROUP
    def far_body(g, carry):
        first = near - FAR_GROUP * (g + 1)
        starts = [first + c * FAR_CHUNK for c in range(FAR_GROUP // FAR_CHUNK)]
        scores = [_bf(_dot(tile(ks_ref, pt, FAR_CHUNK), qt) + chosen(pt, FAR_CHUNK)) for pt in starts]
        m_run = m_ref[...]
        l_run = l_ref[...]
        acc = acc_ref[...]
        for pt, s in zip(starts, scores):
            m_new = jnp.maximum(m_run, jnp.max(s, axis=0, keepdims=True).astype(F32))
            p = jnp.exp(s - _bf(m_new))
            pv = _dot_tn(tile(vs_ref, pt, FAR_CHUNK), p)
            alpha = jnp.exp(m_run - m_new)
            l_run = alpha * l_run + jnp.sum(p.astype(F32), axis=0, keepdims=True)
            acc = alpha * acc + pv
            m_run = m_new
        m_ref[...] = m_run
        l_ref[...] = l_run
        acc_ref[...] = acc
        return carry

    lax.fori_loop(0, ngroup, far_body, 0)

    nw = NSA_WINDOW // qb
    adds = []
    for d in range(nw + 1):
        add = win_ref[d * qb:(d + 1) * qb, :]
        if d < nw:
            add = add + jnp.where(i - nw + d >= 0, 0.0, MASKED)
        adds.append(add)
    s_near = _dot(tile(ks_ref, near, 3), qt) + (chosen(near, 3) + near_ref[...])
    s_win = _dot(tile(kw_ref, i, nw + 1), qt) + jnp.concatenate(adds, axis=0)
    m_old = m_ref[...]
    m_new = jnp.maximum(m_old, jnp.max(s_near, axis=0, keepdims=True))
    p_near = jnp.exp(s_near - m_new)
    pv_near = _dot_tn(tile(vs_ref, near, 3), _bf(p_near))
    p_win = jnp.exp(s_win - jnp.max(s_win, axis=0, keepdims=True))
    pv_win = _dot_tn(tile(vw_ref, i, nw + 1), _bf(p_win))
    alpha = jnp.exp(m_old - m_new)
    l_sel = alpha * l_ref[...] + jnp.sum(p_near, axis=0, keepdims=True)
    o_sel = (alpha * acc_ref[...] + pv_near) / jnp.maximum(l_sel, 1e-30)
    o_win = pv_win / jnp.maximum(jnp.sum(p_win, axis=0, keepdims=True), 1e-30)

    gate = gate_ref[...]
    for h in range(nh):
        sl = slice(h * NSA_DH, (h + 1) * NSA_DH)
        g_sel = _sigmoid(gate[:, nh + h:nh + h + 1])
        g_win = _sigmoid(gate[:, 2 * nh + h:2 * nh + h + 1])
        o_ref[:, sl] = _bf(ocmp_ref[:, sl] + g_sel * o_sel[:, sl].T + g_win * o_win[:, sl].T)


def nsa_sel_win_attention(qn, ks, vs, kw, vw, sel, near_add, win_add, proj, ocmp, batch, seq):
    nq = seq // NSA_QBLOCK
    wide = NSA_HEADS * NSA_QBLOCK
    padded = lambda t: pl.BlockSpec((None, t.shape[1], NSA_DH), lambda b, i: (b, 0, 0))
    table = lambda t: pl.BlockSpec(t.shape, lambda b, i: (0, 0))
    rows = lambda width, cb=0: pl.BlockSpec((NSA_QBLOCK, width), lambda b, i: (b * nq + i, cb))
    return pl.pallas_call(
        _nsa_sel_win_kernel,
        grid=(batch, nq),
        in_specs=[rows(512), padded(ks), padded(vs), padded(kw), padded(vw), rows(LANE), table(near_add),
                  table(win_add), rows(LANE, P_NSA_GATE // LANE), rows(512)],
        out_specs=rows(512),
        out_shape=jax.ShapeDtypeStruct((batch * seq, 512), BF16),
        scratch_shapes=[pltpu.VMEM((SEL_PAD_ROWS + LANE, wide), F32), pltpu.VMEM((1, wide), F32),
                        pltpu.VMEM((1, wide), F32),
                        pltpu.VMEM((NSA_DH, wide), F32)],
        compiler_params=_cparams("parallel", "arbitrary"),
        name="nsa_sel_win_attention",
    )(qn, ks, vs, kw, vw, sel, near_add, win_add, proj, ocmp)


def _t5_bucket_np(dist):
    n = np.maximum(dist, 0)
    max_exact = NUM_BUCKETS // 2
    nf = np.maximum(n, 1).astype(np.float64)
    large = max_exact + (np.log(nf / max_exact) / math.log(MAX_DISTANCE / max_exact)
                         * (NUM_BUCKETS - max_exact)).astype(np.int64)
    large = np.minimum(large, NUM_BUCKETS - 1)
    return np.where(n < max_exact, n, large).astype(np.int32)


def _bias_tables(rel_bias):
    qb = NSA_QBLOCK
    r = np.arange(qb)[:, None]
    l = np.arange(qb)[None, :]
    toep_idx = np.stack([_t5_bucket_np(r - l), _t5_bucket_np(qb + r - l)])
    def lookup(idx):
        onehot = (jnp.asarray(idx.reshape(-1, 1)) == jnp.arange(NUM_BUCKETS)[None, :]).astype(F32)
        return jnp.dot(onehot, rel_bias.astype(F32), precision=HI).reshape(idx.shape + (NSA_HEADS,))

    toep = jnp.transpose(lookup(toep_idx), (3, 0, 1, 2))
    per = qb // NSA_CMP_STRIDE
    band_idx = np.zeros((16, qb, LANE), np.int32)
    for im in range(16):
        base = per * im - 9
        n = base + ((np.arange(LANE) - base) % LANE)
        dist = qb * im + r - NSA_CMP_STRIDE * n[None, :] - (NSA_CMP_LEN - 1)
        band_idx[im] = _t5_bucket_np(dist)
    band = jnp.transpose(lookup(band_idx), (0, 3, 1, 2))
    cfar = rel_bias[NUM_BUCKETS - 1]
    near = jnp.transpose(toep - cfar[:, None, None, None], (1, 3, 0, 2)).reshape(2, qb, NSA_HEADS * qb)
    key = np.arange(qb)[:, None]
    qry = np.tile(np.arange(qb), NSA_HEADS)[None, :]
    diag = jnp.where(key <= qry, near[0], MASKED)
    edge = jnp.asarray(np.where(qry < key, 0.0, MASKED), F32)
    zero = jnp.zeros_like(diag)
    near_add = jnp.concatenate([zero, near[1], diag], axis=0)
    win_add = jnp.concatenate([edge, zero, zero, near[1], diag], axis=0)
    return near_add.astype(F32), win_add.astype(F32), band.astype(F32), cfar.astype(F32)


def _overlap_table(nc, seq):
    n_cmp = (seq - NSA_CMP_LEN) // NSA_CMP_STRIDE + 1
    n_sel = seq // NSA_SEL_BLOCK
    cs = np.arange(nc)[:, None] * NSA_CMP_STRIDE
    ss = np.arange(LANE)[None, :] * NSA_SEL_BLOCK
    ovl = (cs < ss + NSA_SEL_BLOCK) & (cs + NSA_CMP_LEN > ss)
    ovl &= (np.arange(nc)[:, None] < n_cmp) & (np.arange(LANE)[None, :] < n_sel)
    return jnp.asarray(ovl.astype(np.float32), dtype=BF16)


def _pad_axis(t, axis, size):
    pad = [(0, 0)] * t.ndim
    pad[axis] = (0, size - t.shape[axis])
    return jnp.pad(t, pad)


def _pack_w_in(w_in):
    nl, k, _ = w_in.shape
    src = {}
    start = 0
    names = ("gla_q", "gla_k", "gla_v", "gla_g", "gla_lr", "lru_x", "lru_g", "nsa_q", "nsa_kv", "nsa_gate", "rwkv",
             "gates")
    widths = (256, 256, 512, 512, GLA_LOWRANK, 512, 512, 512, 768, 12, 1984, N_BRANCH * D_MODEL)
    for name, wd in zip(names, widths):
        src[name] = (start, wd)
        start += wd
    def piece(name, width):
        s, wd = src[name]
        return _pad_axis(w_in[:, :, s:s + wd], 2, width)
    rs, _ = src["rwkv"]
    rw = jnp.concatenate([
        w_in[:, :, rs:rs + 1536],
        _pad_axis(w_in[:, :, rs + 1536:rs + 1632], 2, LANE),
        _pad_axis(w_in[:, :, rs + 1632:rs + 1728], 2, LANE),
        w_in[:, :, rs + 1728:rs + 1984]], axis=2)
    packed = jnp.concatenate([
        piece("gates", P_GATES_TOTAL), rw, piece("lru_x", 512), piece("lru_g", 512), piece("nsa_q", 512),
        piece("gla_v", 512), piece("gla_g", 512), piece("nsa_kv", 768), piece("gla_q", 256), piece("gla_k", 256),
        piece("gla_lr", LANE), piece("nsa_gate", LANE)], axis=2)
    assert packed.shape[2] == P_GATES_TOTAL + P_MAIN
    return packed.astype(BF16)


def _pack_rwkv_vec(t):
    return jnp.concatenate([t[:, :1536], _pad_axis(t[:, 1536:1632], 1, LANE), _pad_axis(t[:, 1632:1728], 1, LANE),
                            t[:, 1728:1984]], axis=1)[:, None, :]


def _row(t):
    return t[:, None, :]


def kernel(x, rel_bias, attn_norm, ffn_norm, w_in, gla_w_gk, gla_b_gk, gla_out_norm, lru_conv_w, lru_conv_b, lru_w_a, lru_b_a, lru_w_i, lru_b_i, lru_lambda, nsa_cmp_pos, nsa_cmp_k1, nsa_cmp_k2, nsa_cmp_v1, nsa_cmp_v2, nsa_q_norm, nsa_k_norm, rwkv_mu, rwkv_w0, rwkv_w_lora, rwkv_a0, rwkv_a_lora, rwkv_g_lora, rwkv_k_k, rwkv_k_a, rwkv_r_k, rwkv_ln_w, rwkv_ln_b, w_branch, w_out, ffn_up, ffn_conv_w, ffn_conv_b, ffn_down):
    batch, seq, d = x.shape
    depth = w_in.shape[0]
    m = batch * seq
    nc = seq // NSA_CMP_STRIDE
    assert d == D_MODEL and seq % 2048 == 0

    w_in_p = _pack_w_in(w_in)
    w_branch_b = _bf(w_branch)
    w_out_b = _bf(w_out)
    ffn_up_b = _bf(ffn_up)
    ffn_down_b = _bf(ffn_down)
    gla_w_gk_p = _pad_axis(gla_w_gk, 1, LANE)
    lru_conv_w_p = _pad_axis(lru_conv_w, 1, 8)
    ffn_conv_w_p = _pad_axis(ffn_conv_w, 1, 8)
    rwkv_w_lora_p = _bf(_pad_axis(rwkv_w_lora, 1, LANE))
    rwkv_a_lora_p = _bf(_pad_axis(rwkv_a_lora, 1, LANE))
    cmp_pos_p = jnp.broadcast_to(nsa_cmp_pos.reshape(depth, 1, NSA_CMP_LEN * NSA_DH), (depth, 8, NSA_CMP_LEN * NSA_DH))
    near_add, win_add, band, cfar = _bias_tables(rel_bias)
    ovl = _overlap_table(nc, seq)

    tm = min(1024, seq)
    xf = x.reshape(m, d)
    for l in range(depth):
        gates, proj = norm_matmul(xf, _row(attn_norm), w_in_p, l, tm, 1024)
        y_a = gla_mixer(proj, gla_w_gk_p, _row(gla_b_gk), _row(gla_out_norm), l, batch, seq)
        y_b = lru_mixer(proj, lru_conv_w_p, _row(lru_conv_b), _bf(lru_w_a), _row(lru_b_a), _bf(lru_w_i), _row(lru_b_i),
                        _row(lru_lambda), l, batch, seq, min(256, seq))
        qn, ks, vs, kw, vw = nsa_prep(proj, _row(nsa_q_norm), _row(nsa_k_norm), l, min(512, seq))
        kgrp = proj[:, P_NSA_KV:P_NSA_KV + NSA_DH].reshape(batch, nc, NSA_CMP_STRIDE * NSA_DH)
        vgrp = proj[:, P_NSA_KV + NSA_DH:P_NSA_KV + 2 * NSA_DH].reshape(batch, nc, NSA_CMP_STRIDE * NSA_DH)
        kc, vc = nsa_compress(kgrp, vgrp, cmp_pos_p, _bf(nsa_cmp_k1), _bf(nsa_cmp_k2), _bf(nsa_cmp_v1),
                              _bf(nsa_cmp_v2), _row(nsa_k_norm), l)
        ocmp, sel = nsa_cmp_attention(cfar, qn, kc, vc, band, ovl, proj, batch, seq)
        front = lambda t, tiles: jnp.pad(t.reshape(batch, seq, NSA_DH), ((0, 0), (tiles * NSA_QBLOCK, 0), (0, 0)))
        y_c = nsa_sel_win_attention(qn, front(ks, SEL_PAD_TILES), front(vs, SEL_PAD_TILES), front(kw, WIN_PAD_TILES),
                                    front(vw, WIN_PAD_TILES), sel, near_add, win_add, proj, ocmp, batch, seq)
        y_d = rwkv_mixer(proj, _pack_rwkv_vec(rwkv_mu), _row(rwkv_w0), rwkv_w_lora_p, _row(rwkv_a0), rwkv_a_lora_p,
                         _bf(rwkv_g_lora), _row(rwkv_k_k), _row(rwkv_k_a), _row(rwkv_r_k.reshape(depth, RWKV_WIDTH)),
                         _row(rwkv_ln_w), _row(rwkv_ln_b), l, batch, seq)
        merged = merge_branches((y_a, y_b, y_c, y_d), w_branch_b, gates, l, min(512, seq), 512)
        xf = matmul_residual(merged, w_out_b, xf, l, tm, 512)
        act = ffn_up_conv(xf, _row(ffn_norm), ffn_up_b, ffn_conv_w_p, _row(ffn_conv_b), l, seq, tm, 512)
        xf = matmul_residual(act, ffn_down_b, xf, l, tm, 512)
    return xf.reshape(batch, seq, d)
```
